```python
import math
import jax
import jax.numpy as jnp
from jax import lax
import numpy as np

D_MODEL = 4096
BATCH = 2
SEQ = 4096
DEPTH = 2

F32 = jnp.float32
EPS = 1e-6
LB_FLOOR = 1e-30
CHUNK = 64
N_MOD = 6

D_MIX = D_MODEL
D_SSD = D_MIX // 4
D_S5 = D_MIX // 4
D_HGRN = D_MIX // 4
D_SB = D_MIX - D_SSD - D_S5 - D_HGRN

SSD_HEAD_DIM = 64
SSD_HEADS = D_SSD // SSD_HEAD_DIM
SSD_GROUPS = 4
SSD_STATE = 128
SSD_CONV = 4
SSD_CONV_DIM = D_SSD + 2 * SSD_GROUPS * SSD_STATE

S5_GROUP_CH = 16
S5_GROUPS = D_S5 // S5_GROUP_CH
S5_STATE = 64
S5_DT_MIN = 1e-3
S5_DT_MAX = 1e-1

HGRN_EXPAND = 128
HGRN_HEADS = D_HGRN // HGRN_EXPAND
HGRN_HEAD_V = D_HGRN // HGRN_HEADS
HGRN_BLOCK = 16

SB_HEAD_DIM = 128
SB_HEADS = D_SB // SB_HEAD_DIM
SB_QBLOCK = 128

N_EXPERTS = 64
TOP_K = 8
EXPERT_FF = D_MODEL * 3 // 32
SHARED_FF = EXPERT_FF
ROUTED_SCALE = 2.5
MOE_BLOCK = 128

_IN_SIZES = (D_SSD, SSD_CONV_DIM, SSD_HEADS,
             D_S5,
             D_HGRN, D_HGRN, D_HGRN, D_HGRN,
             D_SB, D_SB, D_SB)
_IN_SPLITS = tuple(int(v) for v in np.cumsum(_IN_SIZES)[:-1])
IN_COLS = sum(_IN_SIZES)

kernel_name = "hybrid_stream_encoder_block"


def rmsnorm(x, g):
    xf = x.astype(F32)
    y = xf * lax.rsqrt(jnp.mean(xf * xf, axis=-1, keepdims=True) + EPS)
    return (y * g.astype(F32)).astype(x.dtype)


def causal_dwconv(x, w, b):
    k = w.shape[0]
    y = lax.conv_general_dilated(
        x, w[:, None, :].astype(x.dtype), window_strides=(1,), padding=[(k - 1, 0)],
        dimension_numbers=("NWC", "WIO", "NWC"), feature_group_count=x.shape[-1])
    return y + b


def swiglu(x, wg, wu, wd):
    return (jax.nn.silu(x @ wg) * (x @ wu)) @ wd


def ssd_group(z, xbc, dt_raw, conv_w, conv_b, dt_bias, a_log, d_skip, norm_g):
    bsz, seq, _ = z.shape
    nc = seq // CHUNK
    r = SSD_HEADS // SSD_GROUPS
    xbc = jax.nn.silu(causal_dwconv(xbc, conv_w, conv_b)).astype(F32)
    xs, bm, cm = jnp.split(xbc, [D_SSD, D_SSD + SSD_GROUPS * SSD_STATE], axis=-1)
    xs = xs.reshape(bsz, nc, CHUNK, SSD_GROUPS, r, SSD_HEAD_DIM)
    bm = bm.reshape(bsz, nc, CHUNK, SSD_GROUPS, SSD_STATE)
    cm = cm.reshape(bsz, nc, CHUNK, SSD_GROUPS, SSD_STATE)
    dt = jax.nn.softplus(dt_raw.astype(F32) + dt_bias.astype(F32))
    dt = dt.reshape(bsz, nc, CHUNK, SSD_GROUPS, r)
    a = -jnp.exp(a_log.astype(F32)).reshape(SSD_GROUPS, r)
    a_cs = jnp.cumsum((dt * a).transpose(0, 3, 4, 1, 2), axis=-1)
    xdt = xs * dt[..., None]
    causal = jnp.tril(jnp.ones((CHUNK, CHUNK), bool))
    seg = a_cs[..., :, None] - a_cs[..., None, :]
    decay = jnp.where(causal, jnp.exp(jnp.where(causal, seg, 0.0)), 0.0)
    cb = jnp.einsum("bclgn,bcsgn->bgcls", cm, bm)
    y_diag = jnp.einsum("bgcls,bgrcls,bcsgrp->bclgrp", cb, decay, xdt)
    decay_to_end = jnp.exp(a_cs[..., -1:] - a_cs)
    states = jnp.einsum("bclgn,bgrcl,bclgrp->cbgrpn", bm, decay_to_end, xdt)
    chunk_decay = jnp.exp(a_cs[..., -1]).transpose(3, 0, 1, 2)

    def step(s, inp):
        st, dec = inp
        return s * dec[..., None, None] + st, s

    _, prev = lax.scan(step, jnp.zeros(states.shape[1:], F32), (states, chunk_decay))
    y_off = jnp.einsum("bclgn,cbgrpn,bgrcl->bclgrp", cm, prev, jnp.exp(a_cs))
    y = y_diag + y_off + xs * d_skip.astype(F32).reshape(SSD_GROUPS, r)[:, :, None]
    y = y.reshape(bsz, seq, D_SSD)
    return rmsnorm(y * jax.nn.silu(z.astype(F32)), norm_g)


def _complex_linear_combine(e1, e2):
    a1r, a1i, b1r, b1i = e1
    a2r, a2i, b2r, b2i = e2
    return (a2r * a1r - a2i * a1i, a2r * a1i + a2i * a1r,
            a2r * b1r - a2i * b1i + b2r, a2r * b1i + a2i * b1r + b2i)


def s5_group(u, lam_re, lam_im, log_step, b_re, b_im, c_re, c_im, d_skip, w_glu, b_glu):
    bsz, seq, _ = u.shape
    uf = u.astype(F32)
    ug = uf.reshape(bsz, seq, S5_GROUPS, S5_GROUP_CH)
    lam_re = lam_re.astype(F32)
    lam_im = lam_im.astype(F32)
    step = jnp.exp(log_step.astype(F32))[:, None]
    mag = jnp.exp(lam_re * step)
    ab_re = mag * jnp.cos(lam_im * step)
    ab_im = mag * jnp.sin(lam_im * step)
    den = lam_re * lam_re + lam_im * lam_im
    nr = ab_re - 1.0
    f_re = ((nr * lam_re + ab_im * lam_im) / den)[..., None]
    f_im = ((ab_im * lam_re - nr * lam_im) / den)[..., None]
    b_re = b_re.astype(F32)
    b_im = b_im.astype(F32)
    bb_re = f_re * b_re - f_im * b_im
    bb_im = f_re * b_im + f_im * b_re
    bu_re = jnp.einsum("blgi,gni->blgn", ug, bb_re)
    bu_im = jnp.einsum("blgi,gni->blgn", ug, bb_im)
    a_re = jnp.broadcast_to(ab_re, (1, seq) + ab_re.shape)
    a_im = jnp.broadcast_to(ab_im, (1, seq) + ab_im.shape)
    _, _, h_re, h_im = lax.associative_scan(
        _complex_linear_combine, (a_re, a_im, bu_re, bu_im), axis=1)
    y = (jnp.einsum("blgn,gin->blgi", h_re, c_re.astype(F32))
         - jnp.einsum("blgn,gin->blgi", h_im, c_im.astype(F32)))
    y = y.reshape(bsz, seq, D_S5) + d_skip.astype(F32) * uf
    y = jax.nn.gelu(y)
    return y * jax.nn.sigmoid(y @ w_glu.astype(F32) + b_glu.astype(F32))


def hgrn2_group(q, f, i, g, lb, norm_g):
    bsz, seq, _ = q.shape
    nb = seq // HGRN_BLOCK
    shp_k = (bsz, nb, HGRN_BLOCK, HGRN_HEADS, HGRN_EXPAND)
    shp_v = (bsz, nb, HGRN_BLOCK, HGRN_HEADS, HGRN_HEAD_V)
    qf = jax.nn.silu(q.astype(F32)).reshape(shp_k)
    ff = f.astype(F32)
    lb = lb.astype(F32)
    log_lb = jnp.log(jnp.maximum(lb, LB_FLOOR))
    log_f = jnp.logaddexp(log_lb, jnp.log1p(-lb) + jax.nn.log_sigmoid(ff)).reshape(shp_k)
    k = ((1.0 - lb) * jax.nn.sigmoid(-ff)).reshape(shp_k)
    v = i.astype(F32).reshape(shp_v)
    b_cs = jnp.cumsum(log_f, axis=2)
    q_dec = qf * jnp.exp(b_cs)
    k_end = k * jnp.exp(b_cs[:, :, -1:] - b_cs)
    blk_decay = jnp.exp(b_cs[:, :, -1])
    causal = jnp.tril(jnp.ones((HGRN_BLOCK, HGRN_BLOCK), bool))[:, :, None, None]
    diff = b_cs[:, :, :, None] - b_cs[:, :, None, :]
    pair_decay = jnp.where(causal, jnp.exp(jnp.where(causal, diff, 0.0)), 0.0)
    att = jnp.einsum("bnthk,bnshk,bntshk->bnhts", qf, k, pair_decay)
    o_intra = jnp.einsum("bnhts,bnshv->bnthv", att, v)

    def step(s, inp):
        qd, ke, vv, dec = inp
        o = jnp.einsum("bthk,bhkv->bthv", qd, s)
        s = s * dec[..., None] + jnp.einsum("bthk,bthv->bhkv", ke, vv)
        return s, o

    s0 = jnp.zeros((bsz, HGRN_HEADS, HGRN_EXPAND, HGRN_HEAD_V), F32)
    xs = tuple(jnp.moveaxis(t, 1, 0) for t in (q_dec, k_end, v, blk_decay))
    _, o_inter = lax.scan(step, s0, xs)
    o = (o_intra + jnp.moveaxis(o_inter, 0, 1)).reshape(bsz, seq, HGRN_HEADS, HGRN_HEAD_V)
    o = rmsnorm(o, norm_g).reshape(bsz, seq, D_HGRN)
    return o * jax.nn.silu(g.astype(F32))


def stick_breaking_group(q, k, v):
    bsz, seq, _ = q.shape
    shp = (bsz, seq, SB_HEADS, SB_HEAD_DIM)
    q = q.astype(F32).reshape(shp).transpose(0, 2, 1, 3)
    k = k.astype(F32).reshape(shp).transpose(0, 2, 1, 3)
    v = v.astype(F32).reshape(shp).transpose(0, 2, 1, 3)
    scale = SB_HEAD_DIM ** -0.5
    outs = []
    for blk in range(seq // SB_QBLOCK):
        t0 = blk * SB_QBLOCK
        t1 = t0 + SB_QBLOCK
        z = jnp.einsum("bhtd,bhsd->bhts", q[:, :, t0:t1], k[:, :, :t1]) * scale
        strict = (t0 + jnp.arange(SB_QBLOCK))[:, None] > jnp.arange(t1)[None, :]
        log_keep = jnp.where(strict, jax.nn.log_sigmoid(-z), 0.0)
        between = lax.cumsum(log_keep, axis=3, reverse=True) - log_keep
        w = jnp.where(strict, jnp.exp(jax.nn.log_sigmoid(z) + between), 0.0)
        outs.append(jnp.einsum("bhts,bhsd->bhtd", w, v[:, :, :t1]))
    o = jnp.concatenate(outs, axis=2)
    return o.transpose(0, 2, 1, 3).reshape(bsz, seq, D_SB)


def moe_ffn(h, w_router, e_bias, w_gate, w_up, w_down, ws_gate, ws_up, ws_down, layer):
    bsz, seq, d = h.shape
    n = bsz * seq
    xt = h.reshape(n, d)
    scores = jax.nn.sigmoid((xt @ w_router).astype(F32))
    _, idx = lax.top_k(scores + e_bias.astype(F32), TOP_K)
    gates = jnp.take_along_axis(scores, idx, axis=-1)
    gates = gates / jnp.sum(gates, axis=-1, keepdims=True) * ROUTED_SCALE
    e_flat = idx.reshape(-1)
    tok_flat = jnp.repeat(jnp.arange(n, dtype=jnp.int32), TOP_K)
    g_flat = gates.reshape(-1)
    order = jnp.argsort(e_flat)
    e_sorted = e_flat[order]
    tok_sorted = tok_flat[order]
    g_sorted = g_flat[order]
    counts = jnp.bincount(e_flat, length=N_EXPERTS)
    start = jnp.cumsum(counts) - counts
    padded = (counts + MOE_BLOCK - 1) // MOE_BLOCK * MOE_BLOCK
    pad_end = jnp.cumsum(padded)
    pad_start = pad_end - padded
    dest = pad_start[e_sorted] + jnp.arange(n * TOP_K, dtype=jnp.int32) - start[e_sorted]
    n_blocks = -(-(n * TOP_K + N_EXPERTS * (MOE_BLOCK - 1)) // MOE_BLOCK)
    rows = n_blocks * MOE_BLOCK
    row_tok = jnp.full((rows,), n, jnp.int32).at[dest].set(tok_sorted)
    row_gate = jnp.zeros((rows,), F32).at[dest].set(g_sorted)
    block_start = jnp.arange(n_blocks, dtype=jnp.int32) * MOE_BLOCK
    block_expert = jnp.minimum(jnp.searchsorted(pad_end, block_start, side="right"),
                               N_EXPERTS - 1).astype(jnp.int32)
    x_rows = jnp.concatenate([xt, jnp.zeros((1, d), xt.dtype)], axis=0)

    def expert_block(acc, inp):
        tok, gate, e = inp
        yb = swiglu(x_rows[tok], w_gate[layer, e], w_up[layer, e], w_down[layer, e])
        return acc.at[tok].add(yb.astype(F32) * gate[:, None]), None

    acc, _ = lax.scan(expert_block, jnp.zeros((n + 1, d), F32),
                      (row_tok.reshape(n_blocks, MOE_BLOCK),
                       row_gate.reshape(n_blocks, MOE_BLOCK), block_expert))
    out = acc[:n] + swiglu(xt, ws_gate, ws_up, ws_down).astype(F32)
    return out.reshape(bsz, seq, d).astype(h.dtype)


def setup_inputs(seed: int = 0) -> dict:
    key = jax.random.key(seed)
    keys = jax.random.split(key, 48)
    counter = iter(range(48))

    def nrm(shape, std=1.0):
        return std * jax.random.normal(keys[next(counter)], shape, F32)

    def unif(shape, lo, hi):
        return jax.random.uniform(keys[next(counter)], shape, F32, lo, hi)

    dt0 = jnp.exp(unif((DEPTH, SSD_HEADS), math.log(1e-3), math.log(1e-1)))
    return {
        "x": nrm((BATCH, SEQ, D_MODEL)),
        "c": nrm((BATCH, D_MODEL)),
        "w_ada": nrm((D_MODEL, N_MOD * D_MODEL), 0.5 * D_MODEL ** -0.5),
        "b_ada": nrm((N_MOD * D_MODEL,), 0.01),
        "ada_layer": nrm((DEPTH, N_MOD, D_MODEL), 0.1),
        "norm1": 1.0 + nrm((DEPTH, D_MODEL), 0.02),
        "w_in": nrm((DEPTH, D_MODEL, IN_COLS), D_MODEL ** -0.5),
        "ssd_conv_w": nrm((DEPTH, SSD_CONV, SSD_CONV_DIM), SSD_CONV ** -0.5),
        "ssd_conv_b": nrm((DEPTH, SSD_CONV_DIM), 0.01),
        "ssd_dt_bias": dt0 + jnp.log(-jnp.expm1(-dt0)),
        "ssd_a_log": jnp.log(unif((DEPTH, SSD_HEADS), 1.0, 16.0)),
        "ssd_d": 1.0 + nrm((DEPTH, SSD_HEADS), 0.1),
        "ssd_norm": 1.0 + nrm((DEPTH, D_SSD), 0.02),
        "s5_lam_re": -0.5 + nrm((DEPTH, S5_GROUPS, S5_STATE), 0.01),
        "s5_lam_im": math.pi * jnp.arange(S5_STATE, dtype=F32) + nrm((DEPTH, S5_GROUPS, S5_STATE), 0.01),
        "s5_log_step": unif((DEPTH, S5_GROUPS), math.log(S5_DT_MIN), math.log(S5_DT_MAX)),
        "s5_b_re": nrm((DEPTH, S5_GROUPS, S5_STATE, S5_GROUP_CH), (2 * S5_GROUP_CH) ** -0.5),
        "s5_b_im": nrm((DEPTH, S5_GROUPS, S5_STATE, S5_GROUP_CH), (2 * S5_GROUP_CH) ** -0.5),
        "s5_c_re": nrm((DEPTH, S5_GROUPS, S5_GROUP_CH, S5_STATE), 0.5),
        "s5_c_im": nrm((DEPTH, S5_GROUPS, S5_GROUP_CH, S5_STATE), 0.5),
        "s5_d": nrm((DEPTH, D_S5)),
        "s5_w_glu": nrm((DEPTH, D_S5, D_S5), D_S5 ** -0.5),
        "s5_b_glu": nrm((DEPTH, D_S5), 0.01),
        "s5_norm": 1.0 + nrm((DEPTH, D_S5), 0.02),
        "hgrn_lb_logits": nrm((DEPTH, D_HGRN), 0.1),
        "hgrn_norm": 1.0 + nrm((DEPTH, HGRN_HEAD_V), 0.02),
        "sb_norm": 1.0 + nrm((DEPTH, D_SB), 0.02),
        "w_out": nrm((DEPTH, D_MIX, D_MODEL), D_MIX ** -0.5),
        "norm2": 1.0 + nrm((DEPTH, D_MODEL), 0.02),
        "w_router": nrm((DEPTH, D_MODEL, N_EXPERTS), D_MODEL ** -0.5),
        "e_bias": nrm((DEPTH, N_EXPERTS), 0.01),
        "w_gate": nrm((DEPTH, N_EXPERTS, D_MODEL, EXPERT_FF), D_MODEL ** -0.5),
        "w_up": nrm((DEPTH, N_EXPERTS, D_MODEL, EXPERT_FF), D_MODEL ** -0.5),
        "w_down": nrm((DEPTH, N_EXPERTS, EXPERT_FF, D_MODEL), EXPERT_FF ** -0.5),
        "ws_gate": nrm((DEPTH, D_MODEL, SHARED_FF), D_MODEL ** -0.5),
        "ws_up": nrm((DEPTH, D_MODEL, SHARED_FF), D_MODEL ** -0.5),
        "ws_down": nrm((DEPTH, SHARED_FF, D_MODEL), SHARED_FF ** -0.5),
        "final_norm": 1.0 + nrm((D_MODEL,), 0.02),
    }


def reference(x, c, w_ada, b_ada, ada_layer, norm1, w_in, ssd_conv_w, ssd_conv_b, ssd_dt_bias,
              ssd_a_log, ssd_d, ssd_norm, s5_lam_re, s5_lam_im, s5_log_step, s5_b_re, s5_b_im,
              s5_c_re, s5_c_im, s5_d, s5_w_glu, s5_b_glu, s5_norm, hgrn_lb_logits, hgrn_norm,
              sb_norm, w_out, norm2, w_router, e_bias, w_gate, w_up, w_down, ws_gate, ws_up,
              ws_down, final_norm):
    bsz, _, d = x.shape
    mod = (jax.nn.silu(c) @ w_ada + b_ada).reshape(bsz, N_MOD, d)
    lb_p = jax.nn.softmax(hgrn_lb_logits.astype(F32), axis=0)
    lower_bounds = jnp.cumsum(lb_p, axis=0) - lb_p[0]
    h = x
    for layer in range(DEPTH):
        shift1, scale1, gate1, shift2, scale2, gate2 = [
            (mod[:, j] + ada_layer[layer, j])[:, None, :] for j in range(N_MOD)]
        hn = rmsnorm(h, norm1[layer]) * (1 + scale1) + shift1
        proj = hn @ w_in[layer]
        z, xbc, dt_raw, u, hq, hf, hi, hg, sq, sk, sv = jnp.split(proj, _IN_SPLITS, axis=-1)
        y_ssd = ssd_group(z, xbc, dt_raw, ssd_conv_w[layer], ssd_conv_b[layer],
                          ssd_dt_bias[layer], ssd_a_log[layer], ssd_d[layer], ssd_norm[layer])
        y_s5 = rmsnorm(s5_group(u, s5_lam_re[layer], s5_lam_im[layer], s5_log_step[layer],
                                s5_b_re[layer], s5_b_im[layer], s5_c_re[layer], s5_c_im[layer],
                                s5_d[layer], s5_w_glu[layer], s5_b_glu[layer]), s5_norm[layer])
        y_hgrn = hgrn2_group(hq, hf, hi, hg, lower_bounds[layer], hgrn_norm[layer])
        y_sb = rmsnorm(stick_breaking_group(sq, sk, sv), sb_norm[layer])
        mixed = jnp.concatenate([y_ssd, y_s5, y_hgrn, y_sb], axis=-1).astype(h.dtype)
        h = h + gate1 * (mixed @ w_out[layer])
        hn = rmsnorm(h, norm2[layer]) * (1 + scale2) + shift2
        h = h + gate2 * moe_ffn(hn, w_router[layer], e_bias[layer], w_gate, w_up, w_down,
                                ws_gate[layer], ws_up[layer], ws_down[layer], layer)
    return rmsnorm(h, final_norm)
```

```python
import functools
import math

import jax
import jax.numpy as jnp
import numpy as np
from jax import lax
from jax.experimental import pallas as pl
from jax.experimental.pallas import tpu as pltpu

F32 = jnp.float32
BF16 = jnp.bfloat16
EPS = 1e-6
LB_FLOOR = 1e-30

D_MODEL = 4096
N_MOD = 6
D_GRP = 1024
SSD_HEADS = 16
SSD_HEAD_DIM = 64
SSD_GROUPS = 4
SSD_STATE = 128
SSD_CONV = 4
S5_GROUPS = 64
S5_GROUP_CH = 16
S5_STATE = 64
HGRN_HEADS = 8
SB_HEADS = 8
SB_HEAD_DIM = 128
N_EXPERTS = 64
TOP_K = 8
EXPERT_FF = 384
ROUTED_SCALE = 2.5

LANES = 128
VMEM_LIMIT = 56 * 1024 * 1024

SSD_CHUNK = 128
S5_TILE = 256
S5_SUB = 32
S5_SLAB = 8
HGRN_CHUNK = 128
SB_TQ = 256
SB_TK = 256
MOE_ROWS = 256
COMBINE_TM = 64


def _cparams(sem, vmem=VMEM_LIMIT):
    return pltpu.CompilerParams(dimension_semantics=sem, vmem_limit_bytes=vmem)


def _dot(a, b):
    return jnp.dot(a.astype(BF16), b.astype(BF16), preferred_element_type=F32)


def _dot_nt(a, b):
    return lax.dot_general(a.astype(BF16), b.astype(BF16), (((1,), (1,)), ((), ())),
                           preferred_element_type=F32)


def _dot_tn(a, b):
    return lax.dot_general(a.astype(BF16), b.astype(BF16), (((0,), (0,)), ((), ())),
                           preferred_element_type=F32)


def _split3(x):
    hi = x.astype(BF16)
    r1 = x - hi.astype(F32)
    mid = r1.astype(BF16)
    lo = (r1 - mid.astype(F32)).astype(BF16)
    return hi, mid, lo


def _dot_exact_rhs(m01, x):
    hi, mid, lo = _split3(x)
    return (jnp.dot(m01, hi, preferred_element_type=F32)
            + jnp.dot(m01, mid, preferred_element_type=F32)
            + jnp.dot(m01, lo, preferred_element_type=F32))


def _dot_exact_lhs(x, m01):
    hi, mid, lo = _split3(x)
    return (jnp.dot(hi, m01, preferred_element_type=F32)
            + jnp.dot(mid, m01, preferred_element_type=F32)
            + jnp.dot(lo, m01, preferred_element_type=F32))


def _silu(x):
    return x * jax.nn.sigmoid(x)


def _log_sigmoid(x):
    return jnp.minimum(x, 0.0) - jnp.log1p(jnp.exp(-jnp.abs(x)))


def _ada_kernel(c_ref, w_ref, b_ref, o_ref):
    a = _silu(c_ref[...])
    o_ref[...] = _dot(a, w_ref[...]) + b_ref[...]


def _ada_proj(c, w_ada, b_ada):
    bsz, d = c.shape
    n = w_ada.shape[1]
    rows = 8
    c_pad = jnp.zeros((rows, d), F32).at[:bsz].set(c)
    tn = 512
    out = pl.pallas_call(
        _ada_kernel,
        grid=(n // tn,),
        in_specs=[pl.BlockSpec((rows, d), lambda j: (0, 0)),
                  pl.BlockSpec((d, tn), lambda j: (0, j)),
                  pl.BlockSpec((1, tn), lambda j: (0, j))],
        out_specs=pl.BlockSpec((rows, tn), lambda j: (0, j)),
        out_shape=jax.ShapeDtypeStruct((rows, n), F32),
        compiler_params=_cparams(("arbitrary",)),
        name="ada_proj",
    )(c_pad, w_ada, b_ada.reshape(1, n))
    return out[:bsz]


def _norm_mod_kernel(x_ref, g_ref, sc_ref, sh_ref, o_ref):
    x = x_ref[...]
    y = x * lax.rsqrt(jnp.mean(x * x, axis=-1, keepdims=True) + EPS) * g_ref[...]
    o_ref[...] = (y * (1.0 + sc_ref[0]) + sh_ref[0]).astype(o_ref.dtype)


def _norm_mod(x2, g, scale, shift, seq, out_dtype, tm=256):
    n, d = x2.shape
    bsz = scale.shape[0]
    tpb = seq // tm
    return pl.pallas_call(
        _norm_mod_kernel,
        grid=(n // tm,),
        in_specs=[pl.BlockSpec((tm, d), lambda i: (i, 0)),
                  pl.BlockSpec((1, d), lambda i: (0, 0)),
                  pl.BlockSpec((1, 1, d), lambda i: (i // tpb, 0, 0)),
                  pl.BlockSpec((1, 1, d), lambda i: (i // tpb, 0, 0))],
        out_specs=pl.BlockSpec((tm, d), lambda i: (i, 0)),
        out_shape=jax.ShapeDtypeStruct((n, d), out_dtype),
        compiler_params=_cparams(("arbitrary",)),
        name="norm_mod",
    )(x2, g.reshape(1, d), scale.reshape(bsz, 1, d), shift.reshape(bsz, 1, d))


def _matmul_kernel(a_ref, b_ref, o_ref):
    o_ref[...] = jnp.dot(a_ref[...], b_ref[...], preferred_element_type=F32).astype(o_ref.dtype)


def _matmul(a, b, tm, tn, out_dtype=F32, name="matmul"):
    m, k = a.shape
    n = b.shape[1]
    return pl.pallas_call(
        _matmul_kernel,
        grid=(n // tn, m // tm),
        in_specs=[pl.BlockSpec((tm, k), lambda j, i: (i, 0)),
                  pl.BlockSpec((k, tn), lambda j, i: (0, j))],
        out_specs=pl.BlockSpec((tm, tn), lambda j, i: (i, j)),
        out_shape=jax.ShapeDtypeStruct((m, n), out_dtype),
        compiler_params=_cparams(("arbitrary", "arbitrary")),
        name=name,
    )(a, b)


def _ssd_kernel(z_ref, xs_ref, bc_ref, dt_ref, cw_ref, cb_ref, dtb_ref, a_ref, dsk_ref, ng_ref,
                tri_ref, expand_ref, o_ref, ext_ref, state_ref):
    tc = SSD_CHUNK
    ci = pl.program_id(1)

    @pl.when(ci == 0)
    def _():
        ext_ref[0:8, :] = jnp.zeros((8, 2 * D_GRP), F32)
        state_ref[...] = jnp.zeros_like(state_ref)

    ext_ref[8:8 + tc, 0:D_GRP] = xs_ref[...]
    ext_ref[8:8 + tc, D_GRP:2 * D_GRP] = bc_ref[...]
    conv = cb_ref[...] + cw_ref[3:4, :] * ext_ref[8:8 + tc, :]
    for j in range(1, SSD_CONV):
        conv = conv + cw_ref[3 - j:4 - j, :] * ext_ref[8 - j:8 - j + tc, :]
    ext_ref[0:8, :] = ext_ref[tc:tc + 8, :]
    xbc = _silu(conv)
    xs = xbc[:, 0:D_GRP]
    gs = SSD_GROUPS * SSD_STATE
    bm = xbc[:, D_GRP:D_GRP + gs]
    cm = xbc[:, D_GRP + gs:D_GRP + 2 * gs]

    dt = jax.nn.softplus(dt_ref[...] + dtb_ref[...])
    dta = dt * a_ref[...]
    a_cs = _dot_exact_rhs(tri_ref[...], dta)
    a_cs_t = a_cs.T
    dt_x = _dot_exact_lhs(dt, expand_ref[...])
    acs_x = _dot_exact_lhs(a_cs, expand_ref[...])
    last_x = acs_x[tc - 1:tc, :]
    xdt = xs * dt_x
    xdt_end = xdt * jnp.exp(last_x - acs_x)
    ea_x = jnp.exp(acs_x)
    chunk_decay_x = jnp.exp(last_x)

    row = lax.broadcasted_iota(jnp.int32, (tc, tc), 0)
    col = lax.broadcasted_iota(jnp.int32, (tc, tc), 1)
    causal = row >= col
    lane = lax.broadcasted_iota(jnp.int32, (tc, LANES), 1)
    heads_per_group = SSD_HEADS // SSD_GROUPS
    gw = heads_per_group * SSD_HEAD_DIM

    y_parts = []
    for g in range(SSD_GROUPS):
        bm_g = bm[:, g * SSD_STATE:(g + 1) * SSD_STATE]
        cm_g = cm[:, g * SSD_STATE:(g + 1) * SSD_STATE]
        cb = _dot_nt(cm_g, bm_g)
        st_g = state_ref[:, g * gw:(g + 1) * gw]
        y_off = ea_x[:, g * gw:(g + 1) * gw] * _dot(cm_g, st_g)
        state_ref[:, g * gw:(g + 1) * gw] = (
            st_g * chunk_decay_x[:, g * gw:(g + 1) * gw]
            + _dot_tn(bm_g, xdt_end[:, g * gw:(g + 1) * gw]))
        for pair in range(heads_per_group // 2):
            lo = g * gw + pair * LANES
            x_pair = xdt[:, lo:lo + LANES].astype(BF16)
            ys = []
            for k in range(2):
                h = g * heads_per_group + pair * 2 + k
                seg = a_cs[:, h:h + 1] - a_cs_t[h:h + 1, :]
                decay = jnp.where(causal, jnp.exp(jnp.where(causal, seg, 0.0)), 0.0)
                ys.append(jnp.dot((cb * decay).astype(BF16), x_pair, preferred_element_type=F32))
            y_pair = jnp.where(lane < SSD_HEAD_DIM, ys[0], ys[1])
            y_parts.append(y_pair + y_off[:, pair * LANES:(pair + 1) * LANES])
    y = jnp.concatenate(y_parts, axis=1) + xs * dsk_ref[...]
    y = y * _silu(z_ref[...])
    y = y * lax.rsqrt(jnp.mean(y * y, axis=-1, keepdims=True) + EPS) * ng_ref[...]
    o_ref[...] = y.astype(o_ref.dtype)


def _ssd_mixer(proj, dt_raw, seq, conv_w, conv_b, dt_bias, a_log, d_skip, norm_g):
    n = proj.shape[0]
    bsz = n // seq
    tc = SSD_CHUNK
    nc = seq // tc
    pad = LANES - SSD_HEADS
    dtb = jnp.pad(dt_bias.astype(F32), (0, pad)).reshape(1, LANES)
    a_neg = jnp.pad(-jnp.exp(a_log.astype(F32)), (0, pad)).reshape(1, LANES)
    dsk = jnp.repeat(d_skip.astype(F32), SSD_HEAD_DIM).reshape(1, D_GRP)
    tri = jnp.asarray(np.tril(np.ones((tc, tc), np.float32)), BF16)
    expand = np.zeros((LANES, D_GRP), np.float32)
    for h in range(SSD_HEADS):
        expand[h, h * SSD_HEAD_DIM:(h + 1) * SSD_HEAD_DIM] = 1.0
    expand = jnp.asarray(expand, BF16)
    cdim = 2 * D_GRP
    full = lambda shape: pl.BlockSpec(shape, lambda b, c: (0,) * len(shape))
    return pl.pallas_call(
        _ssd_kernel,
        grid=(bsz, nc),
        in_specs=[pl.BlockSpec((tc, D_GRP), lambda b, c: (b * nc + c, 0)),
                  pl.BlockSpec((tc, D_GRP), lambda b, c: (b * nc + c, 1)),
                  pl.BlockSpec((tc, D_GRP), lambda b, c: (b * nc + c, 2)),
                  pl.BlockSpec((tc, LANES), lambda b, c: (b * nc + c, 0)),
                  full((SSD_CONV, cdim)), full((1, cdim)), full((1, LANES)), full((1, LANES)),
                  full((1, D_GRP)), full((1, D_GRP)), full((tc, tc)), full((LANES, D_GRP))],
        out_specs=pl.BlockSpec((tc, D_GRP), lambda b, c: (b * nc + c, 0)),
        out_shape=jax.ShapeDtypeStruct((n, D_GRP), BF16),
        scratch_shapes=[pltpu.VMEM((tc + 8, cdim), F32),
                        pltpu.VMEM((SSD_STATE, D_GRP), F32)],
        compiler_params=_cparams(("arbitrary", "arbitrary")),
        name="ssd_mixer",
    )(proj, proj, proj, dt_raw, conv_w.astype(F32), conv_b.reshape(1, cdim).astype(F32),
      dtb, a_neg, dsk, norm_g.reshape(1, D_GRP).astype(F32), tri, expand)


def _s5_kernel(u_ref, bre_ref, bim_ref, cre_ref, cim_ref, pw_ref, dsk_ref, wglu_ref, bglu_ref,
               ng_ref, o_ref, hre_ref, him_ref, car_ref, y_ref):
    tt = S5_TILE
    sub = S5_SUB
    sw = S5_SLAB * S5_STATE
    ci = pl.program_id(1)

    @pl.when(ci == 0)
    def _():
        car_ref[...] = jnp.zeros_like(car_ref)

    rows = lax.broadcasted_iota(jnp.int32, (sub, sw), 0)
    n_levels = int(math.log2(sub))
    for s in range(D_GRP // LANES):
        u_s = u_ref[:, s * LANES:(s + 1) * LANES].astype(BF16)
        hre_ref[...] = jnp.dot(u_s, bre_ref[s], preferred_element_type=F32)
        him_ref[...] = jnp.dot(u_s, bim_ref[s], preferred_element_type=F32)
        c_re = car_ref[0:1, s * sw:(s + 1) * sw]
        c_im = car_ref[1:2, s * sw:(s + 1) * sw]
        a1_re = pw_ref[0:1, s * sw:(s + 1) * sw]
        a1_im = pw_ref[1:2, s * sw:(s + 1) * sw]
        for k in range(tt // sub):
            h_re = hre_ref[k * sub:(k + 1) * sub, :]
            h_im = him_ref[k * sub:(k + 1) * sub, :]
            first = rows == 0
            h_re = h_re + jnp.where(first, a1_re * c_re - a1_im * c_im, 0.0)
            h_im = h_im + jnp.where(first, a1_re * c_im + a1_im * c_re, 0.0)
            for lv in range(n_levels):
                d = 1 << lv
                p_re = pw_ref[2 * lv:2 * lv + 1, s * sw:(s + 1) * sw]
                p_im = pw_ref[2 * lv + 1:2 * lv + 2, s * sw:(s + 1) * sw]
                keep = rows >= d
                s_re = jnp.where(keep, pltpu.roll(h_re, d, 0), 0.0)
                s_im = jnp.where(keep, pltpu.roll(h_im, d, 0), 0.0)
                h_re, h_im = (h_re + p_re * s_re - p_im * s_im,
                              h_im + p_re * s_im + p_im * s_re)
            hre_ref[k * sub:(k + 1) * sub, :] = h_re
            him_ref[k * sub:(k + 1) * sub, :] = h_im
            c_re = h_re[sub - 1:sub, :]
            c_im = h_im[sub - 1:sub, :]
        car_ref[0:1, s * sw:(s + 1) * sw] = c_re
        car_ref[1:2, s * sw:(s + 1) * sw] = c_im
        y_ref[:, s * LANES:(s + 1) * LANES] = (_dot(hre_ref[...], cre_ref[s])
                                               - _dot(him_ref[...], cim_ref[s]))
    y = y_ref[...] + dsk_ref[...] * u_ref[...]
    y = jax.nn.gelu(y)
    gate = jax.nn.sigmoid(_dot(y, wglu_ref[...]) + bglu_ref[...])
    y = y * gate
    y = y * lax.rsqrt(jnp.mean(y * y, axis=-1, keepdims=True) + EPS) * ng_ref[...]
    o_ref[...] = y.astype(o_ref.dtype)


def _s5_mixer(proj, seq, lam_re, lam_im, log_step, b_re, b_im, c_re, c_im, d_skip, w_glu, b_glu,
              norm_g):
    n = proj.shape[0]
    bsz = n // seq
    tt = S5_TILE
    nt = seq // tt
    lam_re = lam_re.astype(F32)
    lam_im = lam_im.astype(F32)
    step = jnp.exp(log_step.astype(F32))[:, None]
    mag = jnp.exp(lam_re * step)
    ab_re = mag * jnp.cos(lam_im * step)
    ab_im = mag * jnp.sin(lam_im * step)
    den = lam_re * lam_re + lam_im * lam_im
    nr = ab_re - 1.0
    f_re = ((nr * lam_re + ab_im * lam_im) / den)[..., None]
    f_im = ((ab_im * lam_re - nr * lam_im) / den)[..., None]
    b_re = b_re.astype(F32)
    b_im = b_im.astype(F32)
    bb_re = f_re * b_re - f_im * b_im
    bb_im = f_re * b_im + f_im * b_re
    n_levels = int(math.log2(S5_SUB))
    pw = []
    for lv in range(n_levels):
        d = float(1 << lv)
        m = jnp.exp(d * lam_re * step)
        pw.append((m * jnp.cos(d * lam_im * step)).reshape(-1))
        pw.append((m * jnp.sin(d * lam_im * step)).reshape(-1))
    n_state = S5_GROUPS * S5_STATE
    pw = jnp.stack(pw + [jnp.zeros((n_state,), F32)] * (16 - len(pw)), axis=0)
    n_slab = S5_GROUPS // S5_SLAB
    eye = jnp.eye(S5_SLAB, dtype=F32)

    def in_slabs(bb):
        t = bb.reshape(n_slab, S5_SLAB, S5_STATE, S5_GROUP_CH)
        w = jnp.einsum("sgni,gh->sgihn", t, eye)
        return w.reshape(n_slab, S5_SLAB * S5_GROUP_CH, S5_SLAB * S5_STATE).astype(BF16)

    def out_slabs(cc):
        t = cc.astype(F32).reshape(n_slab, S5_SLAB, S5_GROUP_CH, S5_STATE)
        w = jnp.einsum("sgin,gh->sgnhi", t, eye)
        return w.reshape(n_slab, S5_SLAB * S5_STATE, S5_SLAB * S5_GROUP_CH).astype(BF16)

    sw = S5_SLAB * S5_STATE
    full = lambda shape: pl.BlockSpec(shape, lambda b, c: (0,) * len(shape))
    return pl.pallas_call(
        _s5_kernel,
        grid=(bsz, nt),
        in_specs=[pl.BlockSpec((tt, D_GRP), lambda b, c: (b * nt + c, 3)),
                  full((n_slab, LANES, sw)), full((n_slab, LANES, sw)),
                  full((n_slab, sw, LANES)), full((n_slab, sw, LANES)),
                  full((16, n_state)), full((1, D_GRP)), full((D_GRP, D_GRP)), full((1, D_GRP)),
                  full((1, D_GRP))],
        out_specs=pl.BlockSpec((tt, D_GRP), lambda b, c: (b * nt + c, 0)),
        out_shape=jax.ShapeDtypeStruct((n, D_GRP), BF16),
        scratch_shapes=[pltpu.VMEM((tt, sw), F32), pltpu.VMEM((tt, sw), F32),
                        pltpu.VMEM((8, n_state), F32), pltpu.VMEM((tt, D_GRP), F32)],
        compiler_params=_cparams(("arbitrary", "arbitrary")),
        name="s5_mixer",
    )(proj, in_slabs(bb_re), in_slabs(bb_im), out_slabs(c_re), out_slabs(c_im), pw,
      d_skip.reshape(1, D_GRP).astype(F32), w_glu.astype(BF16),
      b_glu.reshape(1, D_GRP).astype(F32), norm_g.reshape(1, D_GRP).astype(F32))


def _hgrn_level_matrices(c):
    n_lv = int(math.log2(c))
    sums = np.zeros((n_lv, c, c), np.float32)
    masks = np.zeros((n_lv + 1, c, c), np.float32)
    masks[0] = np.eye(c, dtype=np.float32)
    for lv in range(1, n_lv + 1):
        blk = 1 << lv
        half = blk // 2
        for r in range(c):
            base = (r // blk) * blk
            m = base + half - 1
            if r > m:
                sums[lv - 1, r, m + 1:r + 1] = 1.0
                masks[lv, r, base:base + half] = 1.0
            else:
                sums[lv - 1, r, r + 1:m + 1] = 1.0
    return sums.reshape(n_lv * c, c), masks


def _hgrn_kernel(q_ref, f_ref, i_ref, g_ref, lb_ref, ng_ref, tri_ref, sums_ref, masks_ref,
                 o_ref, state_ref):
    c = HGRN_CHUNK
    n_lv = int(math.log2(c))
    ci = pl.program_id(1)

    @pl.when(ci == 0)
    def _():
        state_ref[...] = jnp.zeros_like(state_ref)

    for h in range(HGRN_HEADS):
        sl = slice(h * LANES, (h + 1) * LANES)
        lb = lb_ref[:, sl]
        ff = f_ref[:, sl]
        qf = _silu(q_ref[:, sl])
        log_lb = jnp.log(jnp.maximum(lb, LB_FLOOR))
        log_f = jnp.logaddexp(log_lb, jnp.log1p(-lb) + _log_sigmoid(ff))
        kk = (1.0 - lb) * jax.nn.sigmoid(-ff)
        vv = i_ref[:, sl]
        hi, mid, lo = _split3(log_f)
        pieces = jnp.concatenate([hi, mid, lo], axis=1)
        bcs3 = jnp.dot(tri_ref[...], pieces, preferred_element_type=F32)
        b_cs = bcs3[:, 0:LANES] + bcs3[:, LANES:2 * LANES] + bcs3[:, 2 * LANES:3 * LANES]
        e3 = jnp.dot(sums_ref[...], pieces, preferred_element_type=F32)
        e_all = e3[:, 0:LANES] + e3[:, LANES:2 * LANES] + e3[:, 2 * LANES:3 * LANES]
        b_end = b_cs[c - 1:c, :]
        st = state_ref[h]
        o = _dot_nt(qf * jnp.exp(b_cs), st)
        k_end = kk * jnp.exp(b_end - b_cs)
        state_ref[h] = st * jnp.exp(b_end) + _dot_tn(vv, k_end)
        att = masks_ref[0] * _dot_nt(qf, kk)
        for lv in range(n_lv):
            w = jnp.exp(jnp.minimum(e_all[lv * c:(lv + 1) * c, :], 0.0))
            att = att + masks_ref[lv + 1] * _dot_nt(qf * w, kk * w)
        o = o + _dot(att, vv)
        o = o * lax.rsqrt(jnp.mean(o * o, axis=-1, keepdims=True) + EPS) * ng_ref[...]
        o_ref[:, sl] = (o * _silu(g_ref[:, sl])).astype(o_ref.dtype)


def _hgrn_mixer(proj, seq, lb, norm_g):
    n = proj.shape[0]
    bsz = n // seq
    c = HGRN_CHUNK
    nc = seq // c
    n_lv = int(math.log2(c))
    sums, masks = _hgrn_level_matrices(c)
    tri = jnp.asarray(np.tril(np.ones((c, c), np.float32)), BF16)
    full = lambda shape: pl.BlockSpec(shape, lambda b, k: (0,) * len(shape))
    blk = lambda j: pl.BlockSpec((c, D_GRP), lambda b, k, j=j: (b * nc + k, j))
    return pl.pallas_call(
        _hgrn_kernel,
        grid=(bsz, nc),
        in_specs=[blk(4), blk(5), blk(6), blk(7), full((1, D_GRP)), full((1, LANES)),
                  full((c, c)), full((n_lv * c, c)), full((n_lv + 1, c, c))],
        out_specs=pl.BlockSpec((c, D_GRP), lambda b, k: (b * nc + k, 0)),
        out_shape=jax.ShapeDtypeStruct((n, D_GRP), BF16),
        scratch_shapes=[pltpu.VMEM((HGRN_HEADS, LANES, LANES), F32)],
        compiler_params=_cparams(("arbitrary", "arbitrary")),
        name="hgrn_mixer",
    )(proj, proj, proj, proj, lb.reshape(1, D_GRP).astype(F32),
      norm_g.reshape(1, LANES).astype(F32), tri, jnp.asarray(sums, BF16), jnp.asarray(masks, F32))


def _sb_block(q, kb, vb, upper, run, strict):
    z = _dot_nt(q, kb)
    ls = _log_sigmoid(z)
    lk = ls - z
    if strict is not None:
        lk = jnp.where(strict, lk, 0.0)
    between = run + _dot_exact_lhs(lk, upper)
    w = jnp.exp(ls + between)
    if strict is not None:
        w = jnp.where(strict, w, 0.0)
    return _dot(w, vb), run + jnp.sum(lk, axis=-1, keepdims=True)


def _sb_kernel(q_ref, k_ref, v_ref, upper_ref, o_ref):
    tq, tk = SB_TQ, SB_TK
    qi = pl.program_id(2)
    q = q_ref[...] * (SB_HEAD_DIM ** -0.5)
    upper = upper_ref[...]
    row = lax.broadcasted_iota(jnp.int32, (tq, tk), 0)
    col = lax.broadcasted_iota(jnp.int32, (tq, tk), 1)
    k0 = pl.multiple_of(qi * tq, tq)
    acc, run = _sb_block(q, k_ref[pl.ds(k0, tk), :], v_ref[pl.ds(k0, tk), :], upper,
                         jnp.zeros((tq, 1), F32), row > col)

    def body(jj, carry):
        acc, run = carry
        ks = pl.multiple_of((qi - 1 - jj) * tk, tk)
        d_acc, run = _sb_block(q, k_ref[pl.ds(ks, tk), :], v_ref[pl.ds(ks, tk), :], upper, run,
                               None)
        return acc + d_acc, run

    acc, run = lax.fori_loop(0, qi, body, (acc, run))
    o_ref[...] = acc


def _sb_mixer(proj, seq):
    n = proj.shape[0]
    bsz = n // seq
    tq, tk = SB_TQ, SB_TK
    nq = seq // tq
    upper = jnp.asarray(np.triu(np.ones((tk, tk), np.float32), 1).T, BF16)
    hpg = D_GRP // LANES
    return pl.pallas_call(
        _sb_kernel,
        grid=(bsz, SB_HEADS, nq),
        in_specs=[pl.BlockSpec((tq, LANES), lambda b, h, i: (b * nq + i, 8 * hpg + h)),
                  pl.BlockSpec((seq, LANES), lambda b, h, i: (b, 9 * hpg + h)),
                  pl.BlockSpec((seq, LANES), lambda b, h, i: (b, 10 * hpg + h)),
                  pl.BlockSpec((tk, tk), lambda b, h, i: (0, 0))],
        out_specs=pl.BlockSpec((tq, LANES), lambda b, h, i: (b * nq + i, h)),
        out_shape=jax.ShapeDtypeStruct((n, D_GRP), F32),
        compiler_params=_cparams(("arbitrary", "arbitrary", "arbitrary")),
        name="sb_mixer",
    )(proj, proj, proj, upper)


def _outproj_kernel(p0_ref, p1_ref, p2_ref, p3_ref, sbn_ref, w_ref, h_ref, gate_ref, o_ref):
    sb = p3_ref[...]
    sb = sb * lax.rsqrt(jnp.mean(sb * sb, axis=-1, keepdims=True) + EPS) * sbn_ref[...]
    acc = jnp.dot(p0_ref[...], w_ref[0:D_GRP, :], preferred_element_type=F32)
    acc = acc + jnp.dot(p1_ref[...], w_ref[D_GRP:2 * D_GRP, :], preferred_element_type=F32)
    acc = acc + jnp.dot(p2_ref[...], w_ref[2 * D_GRP:3 * D_GRP, :], preferred_element_type=F32)
    acc = acc + jnp.dot(sb.astype(BF16), w_ref[3 * D_GRP:4 * D_GRP, :],
                        preferred_element_type=F32)
    o_ref[...] = h_ref[...] + gate_ref[0] * acc


def _out_proj(y_ssd, y_s5, y_hgrn, o_sb, sb_norm, w_out, h2, gate, seq, tm=512, tn=1024):
    n, d = h2.shape
    bsz = gate.shape[0]
    tpb = seq // tm
    part = lambda: pl.BlockSpec((tm, D_GRP), lambda j, i: (i, 0))
    return pl.pallas_call(
        _outproj_kernel,
        grid=(d // tn, n // tm),
        in_specs=[part(), part(), part(), part(),
                  pl.BlockSpec((1, D_GRP), lambda j, i: (0, 0)),
                  pl.BlockSpec((4 * D_GRP, tn), lambda j, i: (0, j)),
                  pl.BlockSpec((tm, tn), lambda j, i: (i, j)),
                  pl.BlockSpec((1, 1, tn), lambda j, i: (i // tpb, 0, j))],
        out_specs=pl.BlockSpec((tm, tn), lambda j, i: (i, j)),
        out_shape=jax.ShapeDtypeStruct((n, d), F32),
        compiler_params=_cparams(("arbitrary", "arbitrary")),
        name="out_proj",
    )(y_ssd, y_s5, y_hgrn, o_sb, sb_norm.reshape(1, D_GRP).astype(F32), w_out.astype(BF16), h2,
      gate.reshape(bsz, 1, d))


def _router_kernel(x_ref, w_ref, bias_ref, idx_ref, gate_ref):
    logits = jnp.dot(x_ref[...], w_ref[...], preferred_element_type=F32,
                     precision=lax.Precision.HIGHEST)
    scores = jax.nn.sigmoid(logits)
    tm = scores.shape[0]
    lane = lax.broadcasted_iota(jnp.int32, (tm, LANES), 1)
    sel = jnp.where(lane < N_EXPERTS, scores + bias_ref[...], -jnp.inf)
    idx_out = jnp.zeros((tm, LANES), jnp.int32)
    gate_out = jnp.zeros((tm, LANES), F32)
    total = jnp.zeros((tm, 1), F32)
    for k in range(TOP_K):
        m = jnp.max(sel, axis=-1, keepdims=True)
        am = jnp.min(jnp.where(sel == m, lane, LANES), axis=-1, keepdims=True)
        hit = lane == am
        gk = jnp.sum(jnp.where(hit, scores, 0.0), axis=-1, keepdims=True)
        total = total + gk
        idx_out = jnp.where(lane == k, am, idx_out)
        gate_out = jnp.where(lane == k, gk, gate_out)
        sel = jnp.where(hit, -jnp.inf, sel)
    idx_ref[...] = idx_out
    gate_ref[...] = gate_out / total * ROUTED_SCALE


def _router(hn, w_router, e_bias, tm=256):
    n, d = hn.shape
    w = jnp.zeros((d, LANES), F32).at[:, :N_EXPERTS].set(w_router.astype(F32))
    bias = jnp.zeros((1, LANES), F32).at[0, :N_EXPERTS].set(e_bias.astype(F32))
    idx, gates = pl.pallas_call(
        _router_kernel,
        grid=(n // tm,),
        in_specs=[pl.BlockSpec((tm, d), lambda i: (i, 0)),
                  pl.BlockSpec((d, LANES), lambda i: (0, 0)),
                  pl.BlockSpec((1, LANES), lambda i: (0, 0))],
        out_specs=[pl.BlockSpec((tm, LANES), lambda i: (i, 0)),
                   pl.BlockSpec((tm, LANES), lambda i: (i, 0))],
        out_shape=[jax.ShapeDtypeStruct((n, LANES), jnp.int32),
                   jax.ShapeDtypeStruct((n, LANES), F32)],
        compiler_params=_cparams(("arbitrary",)),
        name="router",
    )(hn, w, bias)
    return idx[:, :TOP_K], gates[:, :TOP_K]


def _expert_kernel(be_ref, tok_ref, nb_ref, x_hbm, wg_ref, wu_ref, wd_ref, y_ref, xbuf, sem):
    rows = MOE_ROWS
    b = pl.program_id(0)
    nb = pl.num_programs(0)

    def gather(blk, slot):
        def issue(r, carry):
            tok = tok_ref[blk * rows + r]
            pltpu.make_async_copy(x_hbm.at[pl.ds(tok, 1), :], xbuf.at[slot, pl.ds(r, 1), :],
                                  sem.at[slot]).start()
            return carry
        lax.fori_loop(0, rows, issue, 0)

    @pl.when(b == 0)
    def _():
        gather(0, 0)

    @pl.when(b + 1 < nb)
    def _():
        gather(b + 1, (b + 1) % 2)

    slot = b % 2
    pltpu.make_async_copy(x_hbm.at[pl.ds(0, rows), :], xbuf.at[slot], sem.at[slot]).wait()

    @pl.when(b < nb_ref[0])
    def _():
        x = xbuf[slot].astype(BF16)
        g = jnp.dot(x, wg_ref[0], preferred_element_type=F32)
        u = jnp.dot(x, wu_ref[0], preferred_element_type=F32)
        a = (_silu(g) * u).astype(BF16)
        y_ref[...] = jnp.dot(a, wd_ref[0], preferred_element_type=F32)

    @pl.when(b >= nb_ref[0])
    def _():
        y_ref[...] = jnp.zeros_like(y_ref)


def _routed_experts(hn, block_expert, row_tok, n_used, w_gate, w_up, w_down):
    n, d = hn.shape
    rows = MOE_ROWS
    n_blocks = block_expert.shape[0]
    ff = w_gate.shape[-1]
    grid_spec = pltpu.PrefetchScalarGridSpec(
        num_scalar_prefetch=3,
        grid=(n_blocks,),
        in_specs=[pl.BlockSpec(memory_space=pl.ANY),
                  pl.BlockSpec((1, d, ff), lambda b, be, tok, nb: (be[b], 0, 0)),
                  pl.BlockSpec((1, d, ff), lambda b, be, tok, nb: (be[b], 0, 0)),
                  pl.BlockSpec((1, ff, d), lambda b, be, tok, nb: (be[b], 0, 0))],
        out_specs=pl.BlockSpec((rows, d), lambda b, be, tok, nb: (b, 0)),
        scratch_shapes=[pltpu.VMEM((2, rows, d), F32), pltpu.SemaphoreType.DMA((2,))],
    )
    return pl.pallas_call(
        _expert_kernel,
        grid_spec=grid_spec,
        out_shape=jax.ShapeDtypeStruct((n_blocks * rows, d), F32),
        compiler_params=_cparams(("arbitrary",)),
        name="routed_experts",
    )(block_expert, row_tok, n_used, hn, w_gate, w_up, w_down)


def _combine_kernel(pos_ref, hn_ref, gates_ref, h_ref, g2_ref, wsg_ref, wsu_ref, wsd_ref, y_hbm,
                    o_ref, ybuf, sem):
    tm = COMBINE_TM
    i = pl.program_id(0)
    nt = pl.num_programs(0)
    per = tm * TOP_K

    def gather(tile, slot):
        def issue(r, carry):
            p = pos_ref[tile * per + r]
            pltpu.make_async_copy(y_hbm.at[pl.ds(p, 1), :], ybuf.at[slot, pl.ds(r, 1), :],
                                  sem.at[slot]).start()
            return carry
        lax.fori_loop(0, per, issue, 0)

    @pl.when(i == 0)
    def _():
        gather(0, 0)

    @pl.when(i + 1 < nt)
    def _():
        gather(i + 1, (i + 1) % 2)

    x = hn_ref[...].astype(BF16)
    a = (_silu(jnp.dot(x, wsg_ref[...], preferred_element_type=F32))
         * jnp.dot(x, wsu_ref[...], preferred_element_type=F32)).astype(BF16)
    acc = jnp.dot(a, wsd_ref[...], preferred_element_type=F32)

    slot = i % 2
    pltpu.make_async_copy(y_hbm.at[pl.ds(0, per), :], ybuf.at[slot], sem.at[slot]).wait()
    gates = gates_ref[...]
    for k in range(TOP_K):
        acc = acc + gates[:, k:k + 1] * ybuf[slot, k * tm:(k + 1) * tm, :]
    o_ref[...] = h_ref[...] + g2_ref[0] * acc


def _combine(pos_km, hn, gates_pad, h2, gate2, ws_gate, ws_up, ws_down, y_rows, seq):
    n, d = hn.shape
    tm = COMBINE_TM
    bsz = gate2.shape[0]
    tpb = seq // tm
    ff = ws_gate.shape[-1]
    grid_spec = pltpu.PrefetchScalarGridSpec(
        num_scalar_prefetch=1,
        grid=(n // tm,),
        in_specs=[pl.BlockSpec((tm, d), lambda i, pos: (i, 0)),
                  pl.BlockSpec((tm, LANES), lambda i, pos: (i, 0)),
                  pl.BlockSpec((tm, d), lambda i, pos: (i, 0)),
                  pl.BlockSpec((1, 1, d), lambda i, pos: (i // tpb, 0, 0)),
                  pl.BlockSpec((d, ff), lambda i, pos: (0, 0)),
                  pl.BlockSpec((d, ff), lambda i, pos: (0, 0)),
                  pl.BlockSpec((ff, d), lambda i, pos: (0, 0)),
                  pl.BlockSpec(memory_space=pl.ANY)],
        out_specs=pl.BlockSpec((tm, d), lambda i, pos: (i, 0)),
        scratch_shapes=[pltpu.VMEM((2, tm * TOP_K, d), F32), pltpu.SemaphoreType.DMA((2,))],
    )
    return pl.pallas_call(
        _combine_kernel,
        grid_spec=grid_spec,
        out_shape=jax.ShapeDtypeStruct((n, d), F32),
        compiler_params=_cparams(("arbitrary",)),
        name="moe_combine",
    )(pos_km, hn, gates_pad, h2, gate2.reshape(bsz, 1, d), ws_gate.astype(BF16),
      ws_up.astype(BF16), ws_down.astype(BF16), y_rows)


def _moe(hn, h2, gate2, seq, w_router, e_bias, w_gate, w_up, w_down, ws_gate, ws_up, ws_down):
    n, d = hn.shape
    rows = MOE_ROWS
    idx, gates = _router(hn, w_router, e_bias)
    e_flat = idx.reshape(-1)
    order = jnp.argsort(e_flat)
    e_sorted = e_flat[order]
    counts = jnp.bincount(e_flat, length=N_EXPERTS)
    start = jnp.cumsum(counts) - counts
    padded = (counts + rows - 1) // rows * rows
    pad_end = jnp.cumsum(padded)
    pad_start = pad_end - padded
    dest = (pad_start[e_sorted] + jnp.arange(n * TOP_K, dtype=jnp.int32)
            - start[e_sorted]).astype(jnp.int32)
    n_blocks = -(-(n * TOP_K + N_EXPERTS * (rows - 1)) // rows)
    row_tok = jnp.zeros((n_blocks * rows,), jnp.int32).at[dest].set((order // TOP_K).astype(jnp.int32))
    pos_flat = jnp.zeros((n * TOP_K,), jnp.int32).at[order].set(dest)
    block_start = jnp.arange(n_blocks, dtype=jnp.int32) * rows
    block_expert = jnp.minimum(jnp.searchsorted(pad_end, block_start, side="right"),
                               N_EXPERTS - 1).astype(jnp.int32)
    n_used = (pad_end[-1] // rows).astype(jnp.int32).reshape(1)
    y_rows = _routed_experts(hn, block_expert, row_tok, n_used, w_gate, w_up, w_down)
    tm = COMBINE_TM
    pos_km = pos_flat.reshape(n // tm, tm, TOP_K).transpose(0, 2, 1).reshape(-1)
    gates_pad = jnp.zeros((n, LANES), F32).at[:, :TOP_K].set(gates)
    return _combine(pos_km, hn, gates_pad, h2, gate2, ws_gate, ws_up, ws_down, y_rows, seq)


def _in_proj_weights(w_in_l):
    dt0 = 3 * D_GRP
    main = jnp.concatenate([w_in_l[:, :dt0], w_in_l[:, dt0 + SSD_HEADS:]], axis=1).astype(BF16)
    dtw = jnp.zeros((w_in_l.shape[0], LANES), BF16).at[:, :SSD_HEADS].set(
        w_in_l[:, dt0:dt0 + SSD_HEADS].astype(BF16))
    return main, dtw


def kernel(x, c, w_ada, b_ada, ada_layer, norm1, w_in, ssd_conv_w, ssd_conv_b, ssd_dt_bias, ssd_a_log, ssd_d, ssd_norm, s5_lam_re, s5_lam_im, s5_log_step, s5_b_re, s5_b_im, s5_c_re, s5_c_im, s5_d, s5_w_glu, s5_b_glu, s5_norm, hgrn_lb_logits, hgrn_norm, sb_norm, w_out, norm2, w_router, e_bias, w_gate, w_up, w_down, ws_gate, ws_up, ws_down, final_norm):
    bsz, seq, d = x.shape
    n = bsz * seq
    depth = w_in.shape[0]
    mod = _ada_proj(c, w_ada, b_ada).reshape(bsz, N_MOD, d)
    lb_p = jax.nn.softmax(hgrn_lb_logits.astype(F32), axis=0)
    lower_bounds = jnp.cumsum(lb_p, axis=0) - lb_p[0]
    h = x.reshape(n, d)
    for layer in range(depth):
        shift1, scale1, gate1, shift2, scale2, gate2 = [
            mod[:, j] + ada_layer[layer, j] for j in range(N_MOD)]
        hn = _norm_mod(h, norm1[layer], scale1, shift1, seq, BF16)
        w_main, w_dt = _in_proj_weights(w_in[layer])
        proj = _matmul(hn, w_main, 512, 1024, name="in_proj")
        dt_raw = _matmul(hn, w_dt, 512, LANES, name="in_proj_dt")
        y_ssd = _ssd_mixer(proj, dt_raw, seq, ssd_conv_w[layer], ssd_conv_b[layer],
                           ssd_dt_bias[layer], ssd_a_log[layer], ssd_d[layer], ssd_norm[layer])
        y_s5 = _s5_mixer(proj, seq, s5_lam_re[layer], s5_lam_im[layer], s5_log_step[layer],
                         s5_b_re[layer], s5_b_im[layer], s5_c_re[layer], s5_c_im[layer],
                         s5_d[layer], s5_w_glu[layer], s5_b_glu[layer], s5_norm[layer])
        y_hgrn = _hgrn_mixer(proj, seq, lower_bounds[layer], hgrn_norm[layer])
        o_sb = _sb_mixer(proj, seq)
        h = _out_proj(y_ssd, y_s5, y_hgrn, o_sb, sb_norm[layer], w_out[layer], h, gate1, seq)
        hn2 = _norm_mod(h, norm2[layer], scale2, shift2, seq, F32)
        h = _moe(hn2, h, gate2, seq, w_router[layer], e_bias[layer],
                 w_gate[layer].astype(BF16), w_up[layer].astype(BF16),
                 w_down[layer].astype(BF16), ws_gate[layer], ws_up[layer], ws_down[layer])
    zeros = jnp.zeros((bsz, d), F32)
    out = _norm_mod(h, final_norm, zeros, zeros, seq, F32)
    return out.reshape(bsz, seq, d)
```

```python
import functools
import math

import jax
import jax.numpy as jnp
import numpy as np
from jax import lax
from jax.experimental import pallas as pl
from jax.experimental.pallas import tpu as pltpu

F32 = jnp.float32
BF16 = jnp.bfloat16
EPS = 1e-6
LB_FLOOR = 1e-30

D_MODEL = 4096
N_MOD = 6
D_GRP = 1024
SSD_HEADS = 16
SSD_HEAD_DIM = 64
SSD_GROUPS = 4
SSD_STATE = 128
SSD_CONV = 4
S5_GROUPS = 64
S5_GROUP_CH = 16
S5_STATE = 64
HGRN_HEADS = 8
SB_HEADS = 8
SB_HEAD_DIM = 128
N_EXPERTS = 64
TOP_K = 8
EXPERT_FF = 384
ROUTED_SCALE = 2.5

LANES = 128
VMEM_LIMIT = 56 * 1024 * 1024

SSD_CHUNK = 128
S5_TILE = 256
S5_SUB = 32
S5_SLAB = 8
HGRN_CHUNK = 128
SB_TQ = 256
SB_TK = 256
SB_DEAD_LOG = -104.0
MOE_ROWS = 256
COMBINE_TM = 128
ROUTER_TM = 256
PACK_W = D_MODEL // 2
PACK_S = PACK_W // LANES
TOK_BITS = 13
BUF_PITCH = 24
SUBLANES = 8


def _pack_rows(y):
    bits = pltpu.bitcast(y.astype(BF16).astype(F32), jnp.uint32)
    return (bits[:, :PACK_W] >> 16) | (bits[:, PACK_W:] & jnp.uint32(0xFFFF0000))


def _unpack_words(w):
    lo = pltpu.bitcast(w << 16, F32)
    hi = pltpu.bitcast(w & jnp.uint32(0xFFFF0000), F32)
    return lo, hi


def _cparams(sem, vmem=VMEM_LIMIT):
    return pltpu.CompilerParams(dimension_semantics=sem, vmem_limit_bytes=vmem)


def _dot(a, b):
    return jnp.dot(a.astype(BF16), b.astype(BF16), preferred_element_type=F32)


def _dot_nt(a, b):
    return lax.dot_general(a.astype(BF16), b.astype(BF16), (((1,), (1,)), ((), ())),
                           preferred_element_type=F32)


def _dot_tn(a, b):
    return lax.dot_general(a.astype(BF16), b.astype(BF16), (((0,), (0,)), ((), ())),
                           preferred_element_type=F32)


def _split3(x):
    hi = x.astype(BF16)
    r1 = x - hi.astype(F32)
    mid = r1.astype(BF16)
    lo = (r1 - mid.astype(F32)).astype(BF16)
    return hi, mid, lo


def _dot_exact_rhs(m01, x):
    hi, mid, lo = _split3(x)
    return (jnp.dot(m01, hi, preferred_element_type=F32)
            + jnp.dot(m01, mid, preferred_element_type=F32)
            + jnp.dot(m01, lo, preferred_element_type=F32))


def _dot_exact_lhs(x, m01):
    hi, mid, lo = _split3(x)
    return (jnp.dot(hi, m01, preferred_element_type=F32)
            + jnp.dot(mid, m01, preferred_element_type=F32)
            + jnp.dot(lo, m01, preferred_element_type=F32))


def _silu(x):
    return x * jax.nn.sigmoid(x)


def _log_sigmoid(x):
    return jnp.minimum(x, 0.0) - jnp.log1p(jnp.exp(-jnp.abs(x)))


def _ada_kernel(c_ref, w_ref, b_ref, o_ref):
    a = _silu(c_ref[...])
    o_ref[...] = _dot(a, w_ref[...]) + b_ref[...]


def _ada_proj(c, w_ada, b_ada):
    bsz, d = c.shape
    n = w_ada.shape[1]
    rows = 8
    c_pad = jnp.zeros((rows, d), F32).at[:bsz].set(c)
    tn = 512
    out = pl.pallas_call(
        _ada_kernel,
        grid=(n // tn,),
        in_specs=[pl.BlockSpec((rows, d), lambda j: (0, 0)),
                  pl.BlockSpec((d, tn), lambda j: (0, j)),
                  pl.BlockSpec((1, tn), lambda j: (0, j))],
        out_specs=pl.BlockSpec((rows, tn), lambda j: (0, j)),
        out_shape=jax.ShapeDtypeStruct((rows, n), F32),
        compiler_params=_cparams(("arbitrary",)),
        name="ada_proj",
    )(c_pad, w_ada, b_ada.reshape(1, n))
    return out[:bsz]


def _norm_mod_kernel(x_ref, g_ref, sc_ref, sh_ref, o_ref):
    x = x_ref[...]
    y = x * lax.rsqrt(jnp.mean(x * x, axis=-1, keepdims=True) + EPS) * g_ref[...]
    o_ref[...] = (y * (1.0 + sc_ref[0]) + sh_ref[0]).astype(o_ref.dtype)


def _norm_mod(x2, g, scale, shift, seq, out_dtype, tm=256):
    n, d = x2.shape
    bsz = scale.shape[0]
    tpb = seq // tm
    return pl.pallas_call(
        _norm_mod_kernel,
        grid=(n // tm,),
        in_specs=[pl.BlockSpec((tm, d), lambda i: (i, 0)),
                  pl.BlockSpec((1, d), lambda i: (0, 0)),
                  pl.BlockSpec((1, 1, d), lambda i: (i // tpb, 0, 0)),
                  pl.BlockSpec((1, 1, d), lambda i: (i // tpb, 0, 0))],
        out_specs=pl.BlockSpec((tm, d), lambda i: (i, 0)),
        out_shape=jax.ShapeDtypeStruct((n, d), out_dtype),
        compiler_params=_cparams(("arbitrary",)),
        name="norm_mod",
    )(x2, g.reshape(1, d), scale.reshape(bsz, 1, d), shift.reshape(bsz, 1, d))


def _matmul_kernel(a_ref, b_ref, o_ref):
    o_ref[...] = jnp.dot(a_ref[...], b_ref[...], preferred_element_type=F32).astype(o_ref.dtype)


def _matmul(a, b, tm, tn, out_dtype=F32, name="matmul"):
    m, k = a.shape
    n = b.shape[1]
    return pl.pallas_call(
        _matmul_kernel,
        grid=(n // tn, m // tm),
        in_specs=[pl.BlockSpec((tm, k), lambda j, i: (i, 0)),
                  pl.BlockSpec((k, tn), lambda j, i: (0, j))],
        out_specs=pl.BlockSpec((tm, tn), lambda j, i: (i, j)),
        out_shape=jax.ShapeDtypeStruct((m, n), out_dtype),
        compiler_params=_cparams(("arbitrary", "arbitrary")),
        name=name,
    )(a, b)


def _ssd_kernel(z_ref, xs_ref, bc_ref, dt_ref, cw_ref, cb_ref, dtb_ref, a_ref, dsk_ref, ng_ref,
                tri_ref, expand_ref, o_ref, ext_ref, state_ref):
    tc = SSD_CHUNK
    ci = pl.program_id(1)

    @pl.when(ci == 0)
    def _():
        ext_ref[0:8, :] = jnp.zeros((8, 2 * D_GRP), F32)
        state_ref[...] = jnp.zeros_like(state_ref)

    ext_ref[8:8 + tc, 0:D_GRP] = xs_ref[...]
    ext_ref[8:8 + tc, D_GRP:2 * D_GRP] = bc_ref[...]
    conv = cb_ref[...] + cw_ref[3:4, :] * ext_ref[8:8 + tc, :]
    for j in range(1, SSD_CONV):
        conv = conv + cw_ref[3 - j:4 - j, :] * ext_ref[8 - j:8 - j + tc, :]
    ext_ref[0:8, :] = ext_ref[tc:tc + 8, :]
    xbc = _silu(conv)
    xs = xbc[:, 0:D_GRP]
    gs = SSD_GROUPS * SSD_STATE
    bm = xbc[:, D_GRP:D_GRP + gs]
    cm = xbc[:, D_GRP + gs:D_GRP + 2 * gs]

    dt = jax.nn.softplus(dt_ref[...] + dtb_ref[...])
    dta = dt * a_ref[...]
    a_cs = _dot_exact_rhs(tri_ref[...], dta)
    a_cs_t = a_cs.T
    dt_x = _dot_exact_lhs(dt, expand_ref[...])
    acs_x = _dot_exact_lhs(a_cs, expand_ref[...])
    last_x = acs_x[tc - 1:tc, :]
    xdt = xs * dt_x
    xdt_end = xdt * jnp.exp(last_x - acs_x)
    ea_x = jnp.exp(acs_x)
    chunk_decay_x = jnp.exp(last_x)

    row = lax.broadcasted_iota(jnp.int32, (tc, tc), 0)
    col = lax.broadcasted_iota(jnp.int32, (tc, tc), 1)
    causal = row >= col
    lane = lax.broadcasted_iota(jnp.int32, (tc, LANES), 1)
    heads_per_group = SSD_HEADS // SSD_GROUPS
    gw = heads_per_group * SSD_HEAD_DIM

    y_parts = []
    for g in range(SSD_GROUPS):
        bm_g = bm[:, g * SSD_STATE:(g + 1) * SSD_STATE]
        cm_g = cm[:, g * SSD_STATE:(g + 1) * SSD_STATE]
        cb = _dot_nt(cm_g, bm_g)
        st_g = state_ref[:, g * gw:(g + 1) * gw]
        y_off = ea_x[:, g * gw:(g + 1) * gw] * _dot(cm_g, st_g)
        state_ref[:, g * gw:(g + 1) * gw] = (
            st_g * chunk_decay_x[:, g * gw:(g + 1) * gw]
            + _dot_tn(bm_g, xdt_end[:, g * gw:(g + 1) * gw]))
        for pair in range(heads_per_group // 2):
            lo = g * gw + pair * LANES
            x_pair = xdt[:, lo:lo + LANES].astype(BF16)
            ys = []
            for k in range(2):
                h = g * heads_per_group + pair * 2 + k
                seg = a_cs[:, h:h + 1] - a_cs_t[h:h + 1, :]
                decay = jnp.where(causal, jnp.exp(jnp.where(causal, seg, 0.0)), 0.0)
                ys.append(jnp.dot((cb * decay).astype(BF16), x_pair, preferred_element_type=F32))
            y_pair = jnp.where(lane < SSD_HEAD_DIM, ys[0], ys[1])
            y_parts.append(y_pair + y_off[:, pair * LANES:(pair + 1) * LANES])
    y = jnp.concatenate(y_parts, axis=1) + xs * dsk_ref[...]
    y = y * _silu(z_ref[...])
    y = y * lax.rsqrt(jnp.mean(y * y, axis=-1, keepdims=True) + EPS) * ng_ref[...]
    o_ref[...] = y.astype(o_ref.dtype)


def _ssd_mixer(proj, dt_raw, seq, conv_w, conv_b, dt_bias, a_log, d_skip, norm_g):
    n = proj.shape[0]
    bsz = n // seq
    tc = SSD_CHUNK
    nc = seq // tc
    pad = LANES - SSD_HEADS
    dtb = jnp.pad(dt_bias.astype(F32), (0, pad)).reshape(1, LANES)
    a_neg = jnp.pad(-jnp.exp(a_log.astype(F32)), (0, pad)).reshape(1, LANES)
    dsk = jnp.repeat(d_skip.astype(F32), SSD_HEAD_DIM).reshape(1, D_GRP)
    tri = jnp.asarray(np.tril(np.ones((tc, tc), np.float32)), BF16)
    expand = np.zeros((LANES, D_GRP), np.float32)
    for h in range(SSD_HEADS):
        expand[h, h * SSD_HEAD_DIM:(h + 1) * SSD_HEAD_DIM] = 1.0
    expand = jnp.asarray(expand, BF16)
    cdim = 2 * D_GRP
    full = lambda shape: pl.BlockSpec(shape, lambda b, c: (0,) * len(shape))
    return pl.pallas_call(
        _ssd_kernel,
        grid=(bsz, nc),
        in_specs=[pl.BlockSpec((tc, D_GRP), lambda b, c: (b * nc + c, 0)),
                  pl.BlockSpec((tc, D_GRP), lambda b, c: (b * nc + c, 1)),
                  pl.BlockSpec((tc, D_GRP), lambda b, c: (b * nc + c, 2)),
                  pl.BlockSpec((tc, LANES), lambda b, c: (b * nc + c, 0)),
                  full((SSD_CONV, cdim)), full((1, cdim)), full((1, LANES)), full((1, LANES)),
                  full((1, D_GRP)), full((1, D_GRP)), full((tc, tc)), full((LANES, D_GRP))],
        out_specs=pl.BlockSpec((tc, D_GRP), lambda b, c: (b * nc + c, 0)),
        out_shape=jax.ShapeDtypeStruct((n, D_GRP), BF16),
        scratch_shapes=[pltpu.VMEM((tc + 8, cdim), F32),
                        pltpu.VMEM((SSD_STATE, D_GRP), F32)],
        compiler_params=_cparams(("arbitrary", "arbitrary")),
        name="ssd_mixer",
    )(proj, proj, proj, dt_raw, conv_w.astype(F32), conv_b.reshape(1, cdim).astype(F32),
      dtb, a_neg, dsk, norm_g.reshape(1, D_GRP).astype(F32), tri, expand)


def _s5_kernel(u_ref, bre_ref, bim_ref, cre_ref, cim_ref, pw_ref, dsk_ref, wglu_ref, bglu_ref,
               ng_ref, o_ref, hre_ref, him_ref, car_ref, y_ref):
    tt = S5_TILE
    sub = S5_SUB
    sw = S5_SLAB * S5_STATE
    ci = pl.program_id(1)

    @pl.when(ci == 0)
    def _():
        car_ref[...] = jnp.zeros_like(car_ref)

    rows = lax.broadcasted_iota(jnp.int32, (sub, sw), 0)
    n_levels = int(math.log2(sub))
    for s in range(D_GRP // LANES):
        u_s = u_ref[:, s * LANES:(s + 1) * LANES].astype(BF16)
        hre_ref[...] = jnp.dot(u_s, bre_ref[s], preferred_element_type=F32)
        him_ref[...] = jnp.dot(u_s, bim_ref[s], preferred_element_type=F32)
        c_re = car_ref[0:1, s * sw:(s + 1) * sw]
        c_im = car_ref[1:2, s * sw:(s + 1) * sw]
        a1_re = pw_ref[0:1, s * sw:(s + 1) * sw]
        a1_im = pw_ref[1:2, s * sw:(s + 1) * sw]
        for k in range(tt // sub):
            h_re = hre_ref[k * sub:(k + 1) * sub, :]
            h_im = him_ref[k * sub:(k + 1) * sub, :]
            first = rows == 0
            h_re = h_re + jnp.where(first, a1_re * c_re - a1_im * c_im, 0.0)
            h_im = h_im + jnp.where(first, a1_re * c_im + a1_im * c_re, 0.0)
            for lv in range(n_levels):
                d = 1 << lv
                p_re = pw_ref[2 * lv:2 * lv + 1, s * sw:(s + 1) * sw]
                p_im = pw_ref[2 * lv + 1:2 * lv + 2, s * sw:(s + 1) * sw]
                keep = rows >= d
                s_re = jnp.where(keep, pltpu.roll(h_re, d, 0), 0.0)
                s_im = jnp.where(keep, pltpu.roll(h_im, d, 0), 0.0)
                h_re, h_im = (h_re + p_re * s_re - p_im * s_im,
                              h_im + p_re * s_im + p_im * s_re)
            hre_ref[k * sub:(k + 1) * sub, :] = h_re
            him_ref[k * sub:(k + 1) * sub, :] = h_im
            c_re = h_re[sub - 1:sub, :]
            c_im = h_im[sub - 1:sub, :]
        car_ref[0:1, s * sw:(s + 1) * sw] = c_re
        car_ref[1:2, s * sw:(s + 1) * sw] = c_im
        y_ref[:, s * LANES:(s + 1) * LANES] = (_dot(hre_ref[...], cre_ref[s])
                                               - _dot(him_ref[...], cim_ref[s]))
    y = y_ref[...] + dsk_ref[...] * u_ref[...]
    y = jax.nn.gelu(y)
    gate = jax.nn.sigmoid(_dot(y, wglu_ref[...]) + bglu_ref[...])
    y = y * gate
    y = y * lax.rsqrt(jnp.mean(y * y, axis=-1, keepdims=True) + EPS) * ng_ref[...]
    o_ref[...] = y.astype(o_ref.dtype)


def _s5_mixer(proj, seq, lam_re, lam_im, log_step, b_re, b_im, c_re, c_im, d_skip, w_glu, b_glu,
              norm_g):
    n = proj.shape[0]
    bsz = n // seq
    tt = S5_TILE
    nt = seq // tt
    lam_re = lam_re.astype(F32)
    lam_im = lam_im.astype(F32)
    step = jnp.exp(log_step.astype(F32))[:, None]
    mag = jnp.exp(lam_re * step)
    ab_re = mag * jnp.cos(lam_im * step)
    ab_im = mag * jnp.sin(lam_im * step)
    den = lam_re * lam_re + lam_im * lam_im
    nr = ab_re - 1.0
    f_re = ((nr * lam_re + ab_im * lam_im) / den)[..., None]
    f_im = ((ab_im * lam_re - nr * lam_im) / den)[..., None]
    b_re = b_re.astype(F32)
    b_im = b_im.astype(F32)
    bb_re = f_re * b_re - f_im * b_im
    bb_im = f_re * b_im + f_im * b_re
    n_levels = int(math.log2(S5_SUB))
    pw = []
    for lv in range(n_levels):
        d = float(1 << lv)
        m = jnp.exp(d * lam_re * step)
        pw.append((m * jnp.cos(d * lam_im * step)).reshape(-1))
        pw.append((m * jnp.sin(d * lam_im * step)).reshape(-1))
    n_state = S5_GROUPS * S5_STATE
    pw = jnp.stack(pw + [jnp.zeros((n_state,), F32)] * (16 - len(pw)), axis=0)
    n_slab = S5_GROUPS // S5_SLAB
    eye = jnp.eye(S5_SLAB, dtype=F32)

    def in_slabs(bb):
        t = bb.reshape(n_slab, S5_SLAB, S5_STATE, S5_GROUP_CH)
        w = jnp.einsum("sgni,gh->sgihn", t, eye)
        return w.reshape(n_slab, S5_SLAB * S5_GROUP_CH, S5_SLAB * S5_STATE).astype(BF16)

    def out_slabs(cc):
        t = cc.astype(F32).reshape(n_slab, S5_SLAB, S5_GROUP_CH, S5_STATE)
        w = jnp.einsum("sgin,gh->sgnhi", t, eye)
        return w.reshape(n_slab, S5_SLAB * S5_STATE, S5_SLAB * S5_GROUP_CH).astype(BF16)

    sw = S5_SLAB * S5_STATE
    full = lambda shape: pl.BlockSpec(shape, lambda b, c: (0,) * len(shape))
    return pl.pallas_call(
        _s5_kernel,
        grid=(bsz, nt),
        in_specs=[pl.BlockSpec((tt, D_GRP), lambda b, c: (b * nt + c, 3)),
                  full((n_slab, LANES, sw)), full((n_slab, LANES, sw)),
                  full((n_slab, sw, LANES)), full((n_slab, sw, LANES)),
                  full((16, n_state)), full((1, D_GRP)), full((D_GRP, D_GRP)), full((1, D_GRP)),
                  full((1, D_GRP))],
        out_specs=pl.BlockSpec((tt, D_GRP), lambda b, c: (b * nt + c, 0)),
        out_shape=jax.ShapeDtypeStruct((n, D_GRP), BF16),
        scratch_shapes=[pltpu.VMEM((tt, sw), F32), pltpu.VMEM((tt, sw), F32),
                        pltpu.VMEM((8, n_state), F32), pltpu.VMEM((tt, D_GRP), F32)],
        compiler_params=_cparams(("arbitrary", "arbitrary")),
        name="s5_mixer",
    )(proj, in_slabs(bb_re), in_slabs(bb_im), out_slabs(c_re), out_slabs(c_im), pw,
      d_skip.reshape(1, D_GRP).astype(F32), w_glu.astype(BF16),
      b_glu.reshape(1, D_GRP).astype(F32), norm_g.reshape(1, D_GRP).astype(F32))


def _hgrn_level_matrices(c):
    n_lv = int(math.log2(c))
    sums = np.zeros((n_lv, c, c), np.float32)
    masks = np.zeros((n_lv + 1, c, c), np.float32)
    masks[0] = np.eye(c, dtype=np.float32)
    for lv in range(1, n_lv + 1):
        blk = 1 << lv
        half = blk // 2
        for r in range(c):
            base = (r // blk) * blk
            m = base + half - 1
            if r > m:
                sums[lv - 1, r, m + 1:r + 1] = 1.0
                masks[lv, r, base:base + half] = 1.0
            else:
                sums[lv - 1, r, r + 1:m + 1] = 1.0
    return sums.reshape(n_lv * c, c), masks


def _hgrn_kernel(q_ref, f_ref, i_ref, g_ref, lb_ref, ng_ref, tri_ref, sums_ref, masks_ref,
                 o_ref, state_ref):
    c = HGRN_CHUNK
    n_lv = int(math.log2(c))
    ci = pl.program_id(1)

    @pl.when(ci == 0)
    def _():
        state_ref[...] = jnp.zeros_like(state_ref)

    for h in range(HGRN_HEADS):
        sl = slice(h * LANES, (h + 1) * LANES)
        lb = lb_ref[:, sl]
        ff = f_ref[:, sl]
        qf = _silu(q_ref[:, sl])
        log_lb = jnp.log(jnp.maximum(lb, LB_FLOOR))
        log_f = jnp.logaddexp(log_lb, jnp.log1p(-lb) + _log_sigmoid(ff))
        kk = (1.0 - lb) * jax.nn.sigmoid(-ff)
        vv = i_ref[:, sl]
        hi, mid, lo = _split3(log_f)
        pieces = jnp.concatenate([hi, mid, lo], axis=1)
        bcs3 = jnp.dot(tri_ref[...], pieces, preferred_element_type=F32)
        b_cs = bcs3[:, 0:LANES] + bcs3[:, LANES:2 * LANES] + bcs3[:, 2 * LANES:3 * LANES]
        e3 = jnp.dot(sums_ref[...], pieces, preferred_element_type=F32)
        e_all = e3[:, 0:LANES] + e3[:, LANES:2 * LANES] + e3[:, 2 * LANES:3 * LANES]
        b_end = b_cs[c - 1:c, :]
        st = state_ref[h]
        o = _dot_nt(qf * jnp.exp(b_cs), st)
        k_end = kk * jnp.exp(b_end - b_cs)
        state_ref[h] = st * jnp.exp(b_end) + _dot_tn(vv, k_end)
        att = masks_ref[0] * _dot_nt(qf, kk)
        for lv in range(n_lv):
            w = jnp.exp(jnp.minimum(e_all[lv * c:(lv + 1) * c, :], 0.0))
            att = att + masks_ref[lv + 1] * _dot_nt(qf * w, kk * w)
        o = o + _dot(att, vv)
        o = o * lax.rsqrt(jnp.mean(o * o, axis=-1, keepdims=True) + EPS) * ng_ref[...]
        o_ref[:, sl] = (o * _silu(g_ref[:, sl])).astype(o_ref.dtype)


def _hgrn_mixer(proj, seq, lb, norm_g):
    n = proj.shape[0]
    bsz = n // seq
    c = HGRN_CHUNK
    nc = seq // c
    n_lv = int(math.log2(c))
    sums, masks = _hgrn_level_matrices(c)
    tri = jnp.asarray(np.tril(np.ones((c, c), np.float32)), BF16)
    full = lambda shape: pl.BlockSpec(shape, lambda b, k: (0,) * len(shape))
    blk = lambda j: pl.BlockSpec((c, D_GRP), lambda b, k, j=j: (b * nc + k, j))
    return pl.pallas_call(
        _hgrn_kernel,
        grid=(bsz, nc),
        in_specs=[blk(4), blk(5), blk(6), blk(7), full((1, D_GRP)), full((1, LANES)),
                  full((c, c)), full((n_lv * c, c)), full((n_lv + 1, c, c))],
        out_specs=pl.BlockSpec((c, D_GRP), lambda b, k: (b * nc + k, 0)),
        out_shape=jax.ShapeDtypeStruct((n, D_GRP), BF16),
        scratch_shapes=[pltpu.VMEM((HGRN_HEADS, LANES, LANES), F32)],
        compiler_params=_cparams(("arbitrary", "arbitrary")),
        name="hgrn_mixer",
    )(proj, proj, proj, proj, lb.reshape(1, D_GRP).astype(F32),
      norm_g.reshape(1, LANES).astype(F32), tri, jnp.asarray(sums, BF16), jnp.asarray(masks, F32))


def _sb_block(q, kb, vb, upper, run, valid):
    z = _dot_nt(q, kb)
    ls = _log_sigmoid(z)
    lk = ls - z
    if valid is not None:
        lk = jnp.where(valid, lk, 0.0)
    between = run + _dot_exact_lhs(lk, upper)
    w = jnp.exp(ls + between)
    if valid is not None:
        w = jnp.where(valid, w, 0.0)
    return _dot(w, vb), run + jnp.sum(lk, axis=-1, keepdims=True)


def _sb_kernel(q_ref, k_ref, v_ref, upper_ref, o_ref):
    tq, tk = SB_TQ, SB_TK
    qi = pl.program_id(2)
    q = q_ref[...] * (SB_HEAD_DIM ** -0.5)
    upper = upper_ref[...]
    row = lax.broadcasted_iota(jnp.int32, (tq, tk), 0)
    col = lax.broadcasted_iota(jnp.int32, (tq, tk), 1)
    k0 = pl.multiple_of(qi * tq, tq)
    acc, run = _sb_block(q, k_ref[pl.ds(k0, tk), :], v_ref[pl.ds(k0, tk), :], upper,
                         jnp.zeros((tq, 1), F32), row > col)
    k1 = pl.multiple_of(jnp.maximum(qi - 1, 0) * tk, tk)
    d_acc, run = _sb_block(q, k_ref[pl.ds(k1, tk), :], v_ref[pl.ds(k1, tk), :], upper, run, qi > 0)
    acc = acc + d_acc

    def alive(run):
        return (jnp.max(run) > SB_DEAD_LOG).astype(jnp.int32)

    def cond(carry):
        j, go, _, _ = carry
        return jnp.logical_and(j >= 0, go > 0)

    def body(carry):
        j, _, acc, run = carry
        ks = pl.multiple_of(j * tk, tk)
        d_acc, run = _sb_block(q, k_ref[pl.ds(ks, tk), :], v_ref[pl.ds(ks, tk), :], upper, run,
                               None)
        return j - 1, alive(run), acc + d_acc, run

    _, _, acc, _ = lax.while_loop(cond, body, (qi - 2, alive(run), acc, run))
    o_ref[...] = acc


def _sb_mixer(proj, seq):
    n = proj.shape[0]
    bsz = n // seq
    tq, tk = SB_TQ, SB_TK
    nq = seq // tq
    upper = jnp.asarray(np.triu(np.ones((tk, tk), np.float32), 1).T, BF16)
    hpg = D_GRP // LANES
    return pl.pallas_call(
        _sb_kernel,
        grid=(bsz, SB_HEADS, nq),
        in_specs=[pl.BlockSpec((tq, LANES), lambda b, h, i: (b * nq + i, 8 * hpg + h)),
                  pl.BlockSpec((seq, LANES), lambda b, h, i: (b, 9 * hpg + h)),
                  pl.BlockSpec((seq, LANES), lambda b, h, i: (b, 10 * hpg + h)),
                  pl.BlockSpec((tk, tk), lambda b, h, i: (0, 0))],
        out_specs=pl.BlockSpec((tq, LANES), lambda b, h, i: (b * nq + i, h)),
        out_shape=jax.ShapeDtypeStruct((n, D_GRP), F32),
        compiler_params=_cparams(("arbitrary", "arbitrary", "arbitrary")),
        name="sb_mixer",
    )(proj, proj, proj, upper)


def _outproj_kernel(p0_ref, p1_ref, p2_ref, p3_ref, sbn_ref, w_ref, h_ref, gate_ref, o_ref):
    sb = p3_ref[...]
    sb = sb * lax.rsqrt(jnp.mean(sb * sb, axis=-1, keepdims=True) + EPS) * sbn_ref[...]
    acc = jnp.dot(p0_ref[...], w_ref[0:D_GRP, :], preferred_element_type=F32)
    acc = acc + jnp.dot(p1_ref[...], w_ref[D_GRP:2 * D_GRP, :], preferred_element_type=F32)
    acc = acc + jnp.dot(p2_ref[...], w_ref[2 * D_GRP:3 * D_GRP, :], preferred_element_type=F32)
    acc = acc + jnp.dot(sb.astype(BF16), w_ref[3 * D_GRP:4 * D_GRP, :],
                        preferred_element_type=F32)
    o_ref[...] = h_ref[...] + gate_ref[0] * acc


def _out_proj(y_ssd, y_s5, y_hgrn, o_sb, sb_norm, w_out, h2, gate, seq, tm=512, tn=1024):
    n, d = h2.shape
    bsz = gate.shape[0]
    tpb = seq // tm
    part = lambda: pl.BlockSpec((tm, D_GRP), lambda j, i: (i, 0))
    return pl.pallas_call(
        _outproj_kernel,
        grid=(d // tn, n // tm),
        in_specs=[part(), part(), part(), part(),
                  pl.BlockSpec((1, D_GRP), lambda j, i: (0, 0)),
                  pl.BlockSpec((4 * D_GRP, tn), lambda j, i: (0, j)),
                  pl.BlockSpec((tm, tn), lambda j, i: (i, j)),
                  pl.BlockSpec((1, 1, tn), lambda j, i: (i // tpb, 0, j))],
        out_specs=pl.BlockSpec((tm, tn), lambda j, i: (i, j)),
        out_shape=jax.ShapeDtypeStruct((n, d), F32),
        compiler_params=_cparams(("arbitrary", "arbitrary")),
        name="out_proj",
    )(y_ssd, y_s5, y_hgrn, o_sb, sb_norm.reshape(1, D_GRP).astype(F32), w_out.astype(BF16), h2,
      gate.reshape(bsz, 1, d))


def _router_kernel(x_ref, g_ref, sc_ref, sh_ref, w_ref, bias_ref, wsg_ref, wsu_ref, wsd_ref,
                   pk_ref, idx_ref, gate_ref, ysh_ref):
    x = x_ref[...]
    y = x * lax.rsqrt(jnp.mean(x * x, axis=-1, keepdims=True) + EPS) * g_ref[...]
    y = y * (1.0 + sc_ref[0]) + sh_ref[0]
    tm = y.shape[0]
    packed = _pack_rows(y)
    for s in range(PACK_S):
        pk_ref[pl.ds(s, tm, stride=PACK_S), :] = packed[:, s * LANES:(s + 1) * LANES]
    yb = y.astype(BF16)
    a = (_silu(jnp.dot(yb, wsg_ref[...], preferred_element_type=F32))
         * jnp.dot(yb, wsu_ref[...], preferred_element_type=F32)).astype(BF16)
    ysh_ref[...] = jnp.dot(a, wsd_ref[...], preferred_element_type=F32).astype(ysh_ref.dtype)
    logits = jnp.dot(y, w_ref[...], preferred_element_type=F32, precision=lax.Precision.HIGHEST)
    scores = jax.nn.sigmoid(logits)
    lane = lax.broadcasted_iota(jnp.int32, (tm, LANES), 1)
    sel = jnp.where(lane < N_EXPERTS, scores + bias_ref[...], -jnp.inf)
    idx_out = jnp.zeros((tm, LANES), jnp.int32)
    gate_out = jnp.zeros((tm, LANES), F32)
    total = jnp.zeros((tm, 1), F32)
    for k in range(TOP_K):
        m = jnp.max(sel, axis=-1, keepdims=True)
        am = jnp.min(jnp.where(sel == m, lane, LANES), axis=-1, keepdims=True)
        hit = lane == am
        gk = jnp.sum(jnp.where(hit, scores, 0.0), axis=-1, keepdims=True)
        total = total + gk
        idx_out = jnp.where(lane == k, am, idx_out)
        gate_out = jnp.where(lane == k, gk, gate_out)
        sel = jnp.where(hit, -jnp.inf, sel)
    idx_ref[...] = idx_out
    gate_ref[...] = gate_out / total * ROUTED_SCALE


def _norm_router_shared(h2, g, scale, shift, seq, w_router, e_bias, ws_gate, ws_up, ws_down):
    n, d = h2.shape
    tm = ROUTER_TM
    bsz = scale.shape[0]
    tpb = seq // tm
    ff = ws_gate.shape[-1]
    w = jnp.zeros((d, LANES), F32).at[:, :N_EXPERTS].set(w_router.astype(F32))
    bias = jnp.zeros((1, LANES), F32).at[0, :N_EXPERTS].set(e_bias.astype(F32))
    const = lambda shape: pl.BlockSpec(shape, lambda i: (0,) * len(shape))
    return pl.pallas_call(
        _router_kernel,
        grid=(n // tm,),
        in_specs=[pl.BlockSpec((tm, d), lambda i: (i, 0)),
                  const((1, d)),
                  pl.BlockSpec((1, 1, d), lambda i: (i // tpb, 0, 0)),
                  pl.BlockSpec((1, 1, d), lambda i: (i // tpb, 0, 0)),
                  const((d, LANES)), const((1, LANES)),
                  const((d, ff)), const((d, ff)), const((ff, d))],
        out_specs=[pl.BlockSpec((tm * PACK_S, LANES), lambda i: (i, 0)),
                   pl.BlockSpec((tm, LANES), lambda i: (i, 0)),
                   pl.BlockSpec((tm, LANES), lambda i: (i, 0)),
                   pl.BlockSpec((tm, d), lambda i: (i, 0))],
        out_shape=[jax.ShapeDtypeStruct((n * PACK_S, LANES), jnp.uint32),
                   jax.ShapeDtypeStruct((n, LANES), jnp.int32),
                   jax.ShapeDtypeStruct((n, LANES), F32),
                   jax.ShapeDtypeStruct((n, d), BF16)],
        compiler_params=_cparams(("arbitrary",)),
        name="norm_router_shared",
    )(h2, g.reshape(1, d), scale.reshape(bsz, 1, d), shift.reshape(bsz, 1, d), w, bias,
      ws_gate.astype(BF16), ws_up.astype(BF16), ws_down.astype(BF16))


def _expert_kernel(be_ref, tab_ref, x_hbm, wg_ref, wu_ref, wd_ref, y_hbm,
                   xa, xb, xbf, ya, yb, sem_g, sem_s):
    rows = MOE_ROWS
    b = pl.program_id(0)
    nb = pl.num_programs(0)
    tok_mask = (1 << TOK_BITS) - 1

    def gather(blk, buf, sem):
        base = (blk + 1) * rows
        for r in range(rows):
            tok = tab_ref[base + r] & tok_mask
            pltpu.make_async_copy(x_hbm.at[pl.ds(tok * PACK_S, PACK_S), :],
                                  buf.at[pl.ds(r * BUF_PITCH, PACK_S), :], sem).start()

    def scatter(tab_blk, buf, sem):
        base = tab_blk * rows
        for r in range(rows):
            dst = pl.multiple_of((tab_ref[base + r] >> TOK_BITS) * SUBLANES, SUBLANES)
            pltpu.make_async_copy(buf.at[pl.ds(r * BUF_PITCH, PACK_S), :],
                                  y_hbm.at[pl.ds(dst, PACK_S), :], sem).start()

    def wait_in(buf, sem):
        pltpu.make_async_copy(x_hbm.at[pl.ds(0, rows * PACK_S), :],
                              buf.at[pl.ds(0, rows * PACK_S), :], sem).wait()

    def wait_out(buf, sem):
        pltpu.make_async_copy(buf.at[pl.ds(0, rows * PACK_S), :],
                              y_hbm.at[pl.ds(0, rows * PACK_S), :], sem).wait()

    @pl.when(b == 0)
    def _():
        yb[...] = jnp.zeros_like(yb)
        gather(0, xa, sem_g.at[0])

    def step(cur_x, nxt_x, cur_y, prv_y, cur, nxt):
        wait_in(cur_x, sem_g.at[cur])

        @pl.when(b >= 1)
        def _():
            wait_out(cur_y, sem_s.at[cur])

        gather(jnp.minimum(b + 1, nb - 1), nxt_x, sem_g.at[nxt])
        scatter(b, prv_y, sem_s.at[nxt])
        for s in range(PACK_S):
            lo, hi = _unpack_words(cur_x[pl.ds(s, rows, stride=BUF_PITCH), :])
            xbf[:, s * LANES:(s + 1) * LANES] = lo.astype(BF16)
            xbf[:, PACK_W + s * LANES:PACK_W + (s + 1) * LANES] = hi.astype(BF16)
        x = xbf[...]
        g = jnp.dot(x, wg_ref[0, 0], preferred_element_type=F32)
        u = jnp.dot(x, wu_ref[0, 0], preferred_element_type=F32)
        a = (_silu(g) * u).astype(BF16)
        packed = _pack_rows(jnp.dot(a, wd_ref[0, 0], preferred_element_type=F32))
        for s in range(PACK_S):
            cur_y[pl.ds(s, rows, stride=BUF_PITCH), :] = packed[:, s * LANES:(s + 1) * LANES]

        @pl.when(b == nb - 1)
        def _():
            scatter(b + 1, cur_y, sem_s.at[cur])
            wait_out(prv_y, sem_s.at[nxt])
            wait_out(cur_y, sem_s.at[cur])
            wait_in(nxt_x, sem_g.at[nxt])

    @pl.when(b % 2 == 0)
    def _():
        step(xa, xb, ya, yb, 0, 1)

    @pl.when(b % 2 == 1)
    def _():
        step(xb, xa, yb, ya, 1, 0)


def _routed_experts(x_packed, block_expert, table, n_y_rows, layer, w_gate, w_up, w_down):
    rows = MOE_ROWS
    n_blocks = block_expert.shape[0]
    d, ff = w_gate.shape[-2:]
    buf = lambda: pltpu.VMEM((rows * BUF_PITCH, LANES), jnp.uint32)
    grid_spec = pltpu.PrefetchScalarGridSpec(
        num_scalar_prefetch=2,
        grid=(n_blocks,),
        in_specs=[pl.BlockSpec(memory_space=pl.ANY),
                  pl.BlockSpec((1, 1, d, ff), lambda b, be, tab: (layer, be[b], 0, 0)),
                  pl.BlockSpec((1, 1, d, ff), lambda b, be, tab: (layer, be[b], 0, 0)),
                  pl.BlockSpec((1, 1, ff, d), lambda b, be, tab: (layer, be[b], 0, 0))],
        out_specs=pl.BlockSpec(memory_space=pl.ANY),
        scratch_shapes=[buf(), buf(), pltpu.VMEM((rows, d), BF16), buf(), buf(),
                        pltpu.SemaphoreType.DMA((2,)), pltpu.SemaphoreType.DMA((2,))],
    )
    return pl.pallas_call(
        _expert_kernel,
        grid_spec=grid_spec,
        out_shape=jax.ShapeDtypeStruct((n_y_rows, LANES), jnp.uint32),
        compiler_params=_cparams(("arbitrary",)),
        name="routed_experts",
    )(block_expert, table, x_packed, w_gate, w_up, w_down)


def _combine_kernel(*refs):
    y_refs = refs[:TOP_K]
    gates_ref, ysh_ref, h_ref, g2_ref, o_ref = refs[TOP_K:]
    tm = COMBINE_TM
    gates = gates_ref[...]
    gk = [gates[:, k:k + 1] for k in range(TOP_K)]
    for s in range(PACK_S):
        acc_lo = ysh_ref[:, s * LANES:(s + 1) * LANES].astype(F32)
        acc_hi = ysh_ref[:, PACK_W + s * LANES:PACK_W + (s + 1) * LANES].astype(F32)
        for k in range(TOP_K):
            lo, hi = _unpack_words(y_refs[k][pl.ds(s, tm, stride=PACK_S), :])
            acc_lo = acc_lo + gk[k] * lo
            acc_hi = acc_hi + gk[k] * hi
        lo_sl = slice(s * LANES, (s + 1) * LANES)
        hi_sl = slice(PACK_W + s * LANES, PACK_W + (s + 1) * LANES)
        o_ref[:, lo_sl] = h_ref[:, lo_sl] + g2_ref[0, :, lo_sl] * acc_lo
        o_ref[:, hi_sl] = h_ref[:, hi_sl] + g2_ref[0, :, hi_sl] * acc_hi


def _combine(y_rows, gates, ysh, h2, gate2, seq):
    n, d = h2.shape
    tm = COMBINE_TM
    bsz = gate2.shape[0]
    tpb = seq // tm
    nt = n // tm
    slot = lambda k: pl.BlockSpec((tm * PACK_S, LANES), lambda i, k=k: (k * nt + i, 0))
    return pl.pallas_call(
        _combine_kernel,
        grid=(nt,),
        in_specs=[slot(k) for k in range(TOP_K)] + [
                  pl.BlockSpec((tm, LANES), lambda i: (i, 0)),
                  pl.BlockSpec((tm, d), lambda i: (i, 0)),
                  pl.BlockSpec((tm, d), lambda i: (i, 0)),
                  pl.BlockSpec((1, 1, d), lambda i: (i // tpb, 0, 0))],
        out_specs=pl.BlockSpec((tm, d), lambda i: (i, 0)),
        out_shape=jax.ShapeDtypeStruct((n, d), F32),
        compiler_params=_cparams(("arbitrary",)),
        name="moe_combine",
    )(*([y_rows] * TOP_K), gates, ysh, h2, gate2.reshape(bsz, 1, d))


def _routing_tables(idx, n):
    rows = MOE_ROWS
    n_pairs = n * TOP_K
    n_blocks = -(-(n_pairs + N_EXPERTS * (rows - 1)) // rows)
    e_flat = idx.reshape(-1)
    _, order = lax.sort_key_val(e_flat, jnp.arange(n_pairs, dtype=jnp.int32))
    experts = jnp.arange(N_EXPERTS, dtype=jnp.int32)
    counts = jnp.sum((e_flat[:, None] == experts[None, :]).astype(jnp.int32), axis=0)
    start = jnp.cumsum(counts) - counts
    padded = (counts + rows - 1) // rows * rows
    pad_end = jnp.cumsum(padded)
    pad_start = pad_end - padded
    block_start = jnp.arange(n_blocks, dtype=jnp.int32) * rows
    block_expert = jnp.minimum(
        jnp.sum((pad_end[None, :] <= block_start[:, None]).astype(jnp.int32), axis=1), N_EXPERTS - 1)
    onehot = (block_expert[:, None] == experts[None, :]).astype(jnp.int32)
    blk_pad_start = jnp.sum(onehot * pad_start[None, :], axis=1)
    blk_start = jnp.sum(onehot * start[None, :], axis=1)
    blk_count = jnp.sum(onehot * counts[None, :], axis=1)
    r_in = jnp.arange(rows, dtype=jnp.int32)[None, :]
    j = block_start[:, None] + r_in - blk_pad_start[:, None]
    valid = j < blk_count[:, None]
    src = jnp.clip(blk_start[:, None] + j, 0, n_pairs - 1)
    pair = jnp.take(order, src.reshape(-1), axis=0).reshape(n_blocks, rows)
    s8 = PACK_S // SUBLANES
    spare = (n_pairs + (jnp.arange(n_blocks, dtype=jnp.int32)[:, None] % 2) * rows + r_in) * s8
    tok = jnp.where(valid, pair // TOP_K, 0)
    dst = jnp.where(valid, ((pair % TOP_K) * n + pair // TOP_K) * s8, spare)
    table = (dst << TOK_BITS) | tok
    lead = ((n_pairs + rows + r_in) * s8) << TOK_BITS
    table = jnp.concatenate([lead, table], axis=0).reshape(-1).astype(jnp.int32)
    return block_expert.astype(jnp.int32), table, (n_pairs + 2 * rows) * PACK_S


def _moe(h2, norm_g, scale2, shift2, gate2, seq, layer, w_router, e_bias, w_gate, w_up, w_down,
         ws_gate, ws_up, ws_down):
    n, d = h2.shape
    x_packed, idx, gates, ysh = _norm_router_shared(h2, norm_g, scale2, shift2, seq, w_router,
                                                    e_bias, ws_gate, ws_up, ws_down)
    block_expert, table, n_y_rows = _routing_tables(idx[:, :TOP_K], n)
    y_rows = _routed_experts(x_packed, block_expert, table, n_y_rows, layer, w_gate, w_up, w_down)
    return _combine(y_rows, gates, ysh, h2, gate2, seq)


def _in_proj_weights(w_in_l):
    dt0 = 3 * D_GRP
    main = jnp.concatenate([w_in_l[:, :dt0], w_in_l[:, dt0 + SSD_HEADS:]], axis=1).astype(BF16)
    dtw = jnp.zeros((w_in_l.shape[0], LANES), BF16).at[:, :SSD_HEADS].set(
        w_in_l[:, dt0:dt0 + SSD_HEADS].astype(BF16))
    return main, dtw


def kernel(x, c, w_ada, b_ada, ada_layer, norm1, w_in, ssd_conv_w, ssd_conv_b, ssd_dt_bias, ssd_a_log, ssd_d, ssd_norm, s5_lam_re, s5_lam_im, s5_log_step, s5_b_re, s5_b_im, s5_c_re, s5_c_im, s5_d, s5_w_glu, s5_b_glu, s5_norm, hgrn_lb_logits, hgrn_norm, sb_norm, w_out, norm2, w_router, e_bias, w_gate, w_up, w_down, ws_gate, ws_up, ws_down, final_norm):
    bsz, seq, d = x.shape
    n = bsz * seq
    depth = w_in.shape[0]
    mod = _ada_proj(c, w_ada, b_ada).reshape(bsz, N_MOD, d)
    lb_p = jax.nn.softmax(hgrn_lb_logits.astype(F32), axis=0)
    lower_bounds = jnp.cumsum(lb_p, axis=0) - lb_p[0]
    h = x.reshape(n, d)
    w_gate_b, w_up_b, w_down_b = w_gate.astype(BF16), w_up.astype(BF16), w_down.astype(BF16)
    for layer in range(depth):
        shift1, scale1, gate1, shift2, scale2, gate2 = [
            mod[:, j] + ada_layer[layer, j] for j in range(N_MOD)]
        hn = _norm_mod(h, norm1[layer], scale1, shift1, seq, BF16)
        w_main, w_dt = _in_proj_weights(w_in[layer])
        proj = _matmul(hn, w_main, 512, 1024, name="in_proj")
        dt_raw = _matmul(hn, w_dt, 512, LANES, name="in_proj_dt")
        y_ssd = _ssd_mixer(proj, dt_raw, seq, ssd_conv_w[layer], ssd_conv_b[layer],
                           ssd_dt_bias[layer], ssd_a_log[layer], ssd_d[layer], ssd_norm[layer])
        y_s5 = _s5_mixer(proj, seq, s5_lam_re[layer], s5_lam_im[layer], s5_log_step[layer],
                         s5_b_re[layer], s5_b_im[layer], s5_c_re[layer], s5_c_im[layer],
                         s5_d[layer], s5_w_glu[layer], s5_b_glu[layer], s5_norm[layer])
        y_hgrn = _hgrn_mixer(proj, seq, lower_bounds[layer], hgrn_norm[layer])
        o_sb = _sb_mixer(proj, seq)
        h = _out_proj(y_ssd, y_s5, y_hgrn, o_sb, sb_norm[layer], w_out[layer], h, gate1, seq)
        h = _moe(h, norm2[layer], scale2, shift2, gate2, seq, layer, w_router[layer], e_bias[layer],
                 w_gate_b, w_up_b, w_down_b, ws_gate[layer], ws_up[layer], ws_down[layer])
    zeros = jnp.zeros((bsz, d), F32)
    out = _norm_mod(h, final_norm, zeros, zeros, seq, F32)
    return out.reshape(bsz, seq, d)
```

```python
import functools
import math

import jax
import jax.numpy as jnp
import numpy as np
from jax import lax
from jax.experimental import pallas as pl
from jax.experimental.pallas import tpu as pltpu

F32 = jnp.float32
BF16 = jnp.bfloat16
EPS = 1e-6
LB_FLOOR = 1e-30

D_MODEL = 4096
N_MOD = 6
D_GRP = 1024
SSD_HEADS = 16
SSD_HEAD_DIM = 64
SSD_GROUPS = 4
SSD_STATE = 128
SSD_CONV = 4
S5_GROUPS = 64
S5_GROUP_CH = 16
S5_STATE = 64
HGRN_HEADS = 8
SB_HEADS = 8
SB_HEAD_DIM = 128
N_EXPERTS = 64
TOP_K = 8
EXPERT_FF = 384
ROUTED_SCALE = 2.5

LANES = 128
VMEM_LIMIT = 56 * 1024 * 1024
EXPERT_VMEM_LIMIT = 60 * 1024 * 1024

SSD_CHUNK = 128
S5_TILE = 256
S5_SUB = 16
S5_SLAB = 8
HGRN_CHUNK = 128
SB_TQ = 256
SB_TK = 256
SB_DEAD_LOG = -104.0
MOE_ROWS = 256
COMBINE_TM = 128
ROUTER_TM = 256
PACK_W = D_MODEL // 2
PACK_S = PACK_W // LANES
TOK_BITS = 13
BUF_PITCH = 24
SUBLANES = 8


def _pack_rows(y):
    bits = pltpu.bitcast(y.astype(BF16).astype(F32), jnp.uint32)
    return (bits[:, :PACK_W] >> 16) | (bits[:, PACK_W:] & jnp.uint32(0xFFFF0000))


def _unpack_words(w):
    lo = pltpu.bitcast(w << 16, F32)
    hi = pltpu.bitcast(w & jnp.uint32(0xFFFF0000), F32)
    return lo, hi


def _cparams(sem, vmem=VMEM_LIMIT):
    return pltpu.CompilerParams(dimension_semantics=sem, vmem_limit_bytes=vmem)


def _dot(a, b):
    return jnp.dot(a.astype(BF16), b.astype(BF16), preferred_element_type=F32)


def _dot_nt(a, b):
    return lax.dot_general(a.astype(BF16), b.astype(BF16), (((1,), (1,)), ((), ())),
                           preferred_element_type=F32)


def _dot_tn(a, b):
    return lax.dot_general(a.astype(BF16), b.astype(BF16), (((0,), (0,)), ((), ())),
                           preferred_element_type=F32)


def _split3(x):
    hi = x.astype(BF16)
    r1 = x - hi.astype(F32)
    mid = r1.astype(BF16)
    lo = (r1 - mid.astype(F32)).astype(BF16)
    return hi, mid, lo


def _dot_exact_rhs(m01, x):
    hi, mid, lo = _split3(x)
    return (jnp.dot(m01, hi, preferred_element_type=F32)
            + jnp.dot(m01, mid, preferred_element_type=F32)
            + jnp.dot(m01, lo, preferred_element_type=F32))


def _dot_exact_lhs(x, m01):
    hi, mid, lo = _split3(x)
    return (jnp.dot(hi, m01, preferred_element_type=F32)
            + jnp.dot(mid, m01, preferred_element_type=F32)
            + jnp.dot(lo, m01, preferred_element_type=F32))


def _silu(x):
    return x * jax.nn.sigmoid(x)


def _log_sigmoid(x):
    return jnp.minimum(x, 0.0) - jnp.log1p(jnp.exp(-jnp.abs(x)))


def _ada_kernel(c_ref, w_ref, b_ref, o_ref):
    a = _silu(c_ref[...])
    o_ref[...] = _dot(a, w_ref[...]) + b_ref[...]


def _ada_proj(c, w_ada, b_ada):
    bsz, d = c.shape
    n = w_ada.shape[1]
    rows = 8
    c_pad = jnp.zeros((rows, d), F32).at[:bsz].set(c)
    tn = 512
    out = pl.pallas_call(
        _ada_kernel,
        grid=(n // tn,),
        in_specs=[pl.BlockSpec((rows, d), lambda j: (0, 0)),
                  pl.BlockSpec((d, tn), lambda j: (0, j)),
                  pl.BlockSpec((1, tn), lambda j: (0, j))],
        out_specs=pl.BlockSpec((rows, tn), lambda j: (0, j)),
        out_shape=jax.ShapeDtypeStruct((rows, n), F32),
        compiler_params=_cparams(("arbitrary",)),
        name="ada_proj",
    )(c_pad, w_ada, b_ada.reshape(1, n))
    return out[:bsz]


def _norm_mod_kernel(x_ref, g_ref, sc_ref, sh_ref, o_ref):
    x = x_ref[...]
    y = x * lax.rsqrt(jnp.mean(x * x, axis=-1, keepdims=True) + EPS) * g_ref[...]
    o_ref[...] = (y * (1.0 + sc_ref[0]) + sh_ref[0]).astype(o_ref.dtype)


def _norm_mod(x2, g, scale, shift, seq, out_dtype, tm=256):
    n, d = x2.shape
    bsz = scale.shape[0]
    tpb = seq // tm
    return pl.pallas_call(
        _norm_mod_kernel,
        grid=(n // tm,),
        in_specs=[pl.BlockSpec((tm, d), lambda i: (i, 0)),
                  pl.BlockSpec((1, d), lambda i: (0, 0)),
                  pl.BlockSpec((1, 1, d), lambda i: (i // tpb, 0, 0)),
                  pl.BlockSpec((1, 1, d), lambda i: (i // tpb, 0, 0))],
        out_specs=pl.BlockSpec((tm, d), lambda i: (i, 0)),
        out_shape=jax.ShapeDtypeStruct((n, d), out_dtype),
        compiler_params=_cparams(("arbitrary",)),
        name="norm_mod",
    )(x2, g.reshape(1, d), scale.reshape(bsz, 1, d), shift.reshape(bsz, 1, d))


def _matmul_kernel(a_ref, b_ref, o_ref):
    o_ref[...] = jnp.dot(a_ref[...], b_ref[...], preferred_element_type=F32).astype(o_ref.dtype)


def _matmul(a, b, tm, tn, out_dtype=F32, name="matmul"):
    m, k = a.shape
    n = b.shape[1]
    return pl.pallas_call(
        _matmul_kernel,
        grid=(n // tn, m // tm),
        in_specs=[pl.BlockSpec((tm, k), lambda j, i: (i, 0)),
                  pl.BlockSpec((k, tn), lambda j, i: (0, j))],
        out_specs=pl.BlockSpec((tm, tn), lambda j, i: (i, j)),
        out_shape=jax.ShapeDtypeStruct((m, n), out_dtype),
        compiler_params=_cparams(("arbitrary", "arbitrary")),
        name=name,
    )(a, b)


def _ssd_kernel(z_ref, xs_ref, bc_ref, dt_ref, cw_ref, cb_ref, dtb_ref, a_ref, dsk_ref, ng_ref,
                tri_ref, expand_ref, o_ref, ext_ref, state_ref):
    tc = SSD_CHUNK
    ci = pl.program_id(1)

    @pl.when(ci == 0)
    def _():
        ext_ref[0:8, :] = jnp.zeros((8, 2 * D_GRP), F32)
        state_ref[...] = jnp.zeros_like(state_ref)

    ext_ref[8:8 + tc, 0:D_GRP] = xs_ref[...]
    ext_ref[8:8 + tc, D_GRP:2 * D_GRP] = bc_ref[...]
    conv = cb_ref[...] + cw_ref[3:4, :] * ext_ref[8:8 + tc, :]
    for j in range(1, SSD_CONV):
        conv = conv + cw_ref[3 - j:4 - j, :] * ext_ref[8 - j:8 - j + tc, :]
    ext_ref[0:8, :] = ext_ref[tc:tc + 8, :]
    xbc = _silu(conv)
    xs = xbc[:, 0:D_GRP]
    gs = SSD_GROUPS * SSD_STATE
    bm = xbc[:, D_GRP:D_GRP + gs]
    cm = xbc[:, D_GRP + gs:D_GRP + 2 * gs]

    dt = jax.nn.softplus(dt_ref[...] + dtb_ref[...])
    dta = dt * a_ref[...]
    a_cs = _dot_exact_rhs(tri_ref[...], dta)
    a_cs_t = a_cs.T
    dt_x = _dot_exact_lhs(dt, expand_ref[...])
    acs_x = _dot_exact_lhs(a_cs, expand_ref[...])
    last_x = acs_x[tc - 1:tc, :]
    xdt = xs * dt_x
    xdt_end = xdt * jnp.exp(last_x - acs_x)
    ea_x = jnp.exp(acs_x)
    chunk_decay_x = jnp.exp(last_x)

    row = lax.broadcasted_iota(jnp.int32, (tc, tc), 0)
    col = lax.broadcasted_iota(jnp.int32, (tc, tc), 1)
    causal = row >= col
    lane = lax.broadcasted_iota(jnp.int32, (tc, LANES), 1)
    heads_per_group = SSD_HEADS // SSD_GROUPS
    gw = heads_per_group * SSD_HEAD_DIM

    y_parts = []
    for g in range(SSD_GROUPS):
        bm_g = bm[:, g * SSD_STATE:(g + 1) * SSD_STATE]
        cm_g = cm[:, g * SSD_STATE:(g + 1) * SSD_STATE]
        cb = _dot_nt(cm_g, bm_g)
        st_g = state_ref[:, g * gw:(g + 1) * gw]
        y_off = ea_x[:, g * gw:(g + 1) * gw] * _dot(cm_g, st_g)
        state_ref[:, g * gw:(g + 1) * gw] = (
            st_g * chunk_decay_x[:, g * gw:(g + 1) * gw]
            + _dot_tn(bm_g, xdt_end[:, g * gw:(g + 1) * gw]))
        for pair in range(heads_per_group // 2):
            lo = g * gw + pair * LANES
            x_pair = xdt[:, lo:lo + LANES].astype(BF16)
            ys = []
            for k in range(2):
                h = g * heads_per_group + pair * 2 + k
                seg = a_cs[:, h:h + 1] - a_cs_t[h:h + 1, :]
                decay = jnp.where(causal, jnp.exp(jnp.where(causal, seg, 0.0)), 0.0)
                ys.append(jnp.dot((cb * decay).astype(BF16), x_pair, preferred_element_type=F32))
            y_pair = jnp.where(lane < SSD_HEAD_DIM, ys[0], ys[1])
            y_parts.append(y_pair + y_off[:, pair * LANES:(pair + 1) * LANES])
    y = jnp.concatenate(y_parts, axis=1) + xs * dsk_ref[...]
    y = y * _silu(z_ref[...])
    y = y * lax.rsqrt(jnp.mean(y * y, axis=-1, keepdims=True) + EPS) * ng_ref[...]
    o_ref[...] = y.astype(o_ref.dtype)


def _ssd_mixer(proj, dt_raw, seq, conv_w, conv_b, dt_bias, a_log, d_skip, norm_g):
    n = proj.shape[0]
    bsz = n // seq
    tc = SSD_CHUNK
    nc = seq // tc
    pad = LANES - SSD_HEADS
    dtb = jnp.pad(dt_bias.astype(F32), (0, pad)).reshape(1, LANES)
    a_neg = jnp.pad(-jnp.exp(a_log.astype(F32)), (0, pad)).reshape(1, LANES)
    dsk = jnp.repeat(d_skip.astype(F32), SSD_HEAD_DIM).reshape(1, D_GRP)
    tri = jnp.asarray(np.tril(np.ones((tc, tc), np.float32)), BF16)
    expand = np.zeros((LANES, D_GRP), np.float32)
    for h in range(SSD_HEADS):
        expand[h, h * SSD_HEAD_DIM:(h + 1) * SSD_HEAD_DIM] = 1.0
    expand = jnp.asarray(expand, BF16)
    cdim = 2 * D_GRP
    full = lambda shape: pl.BlockSpec(shape, lambda b, c: (0,) * len(shape))
    return pl.pallas_call(
        _ssd_kernel,
        grid=(bsz, nc),
        in_specs=[pl.BlockSpec((tc, D_GRP), lambda b, c: (b * nc + c, 0)),
                  pl.BlockSpec((tc, D_GRP), lambda b, c: (b * nc + c, 1)),
                  pl.BlockSpec((tc, D_GRP), lambda b, c: (b * nc + c, 2)),
                  pl.BlockSpec((tc, LANES), lambda b, c: (b * nc + c, 0)),
                  full((SSD_CONV, cdim)), full((1, cdim)), full((1, LANES)), full((1, LANES)),
                  full((1, D_GRP)), full((1, D_GRP)), full((tc, tc)), full((LANES, D_GRP))],
        out_specs=pl.BlockSpec((tc, D_GRP), lambda b, c: (b * nc + c, 0)),
        out_shape=jax.ShapeDtypeStruct((n, D_GRP), BF16),
        scratch_shapes=[pltpu.VMEM((tc + 8, cdim), F32),
                        pltpu.VMEM((SSD_STATE, D_GRP), F32)],
        compiler_params=_cparams(("arbitrary", "arbitrary")),
        name="ssd_mixer",
    )(proj, proj, proj, dt_raw, conv_w.astype(F32), conv_b.reshape(1, cdim).astype(F32),
      dtb, a_neg, dsk, norm_g.reshape(1, D_GRP).astype(F32), tri, expand)


def _s5_kernel(u_ref, bre_ref, bim_ref, cre_ref, cim_ref, pw_ref, dsk_ref, wglu_ref, bglu_ref,
               ng_ref, o_ref, hre_ref, him_ref, car_ref, y_ref):
    tt = S5_TILE
    sub = S5_SUB
    sw = S5_SLAB * S5_STATE
    ci = pl.program_id(1)

    @pl.when(ci == 0)
    def _():
        car_ref[...] = jnp.zeros_like(car_ref)

    rows = lax.broadcasted_iota(jnp.int32, (sub, sw), 0)
    n_levels = int(math.log2(sub))
    for s in range(D_GRP // LANES):
        u_s = u_ref[:, s * LANES:(s + 1) * LANES].astype(BF16)
        hre_ref[...] = jnp.dot(u_s, bre_ref[s], preferred_element_type=F32)
        him_ref[...] = jnp.dot(u_s, bim_ref[s], preferred_element_type=F32)
        c_re = car_ref[0:1, s * sw:(s + 1) * sw]
        c_im = car_ref[1:2, s * sw:(s + 1) * sw]
        a1_re = pw_ref[0:1, s * sw:(s + 1) * sw]
        a1_im = pw_ref[1:2, s * sw:(s + 1) * sw]
        for k in range(tt // sub):
            h_re = hre_ref[k * sub:(k + 1) * sub, :]
            h_im = him_ref[k * sub:(k + 1) * sub, :]
            first = rows == 0
            h_re = h_re + jnp.where(first, a1_re * c_re - a1_im * c_im, 0.0)
            h_im = h_im + jnp.where(first, a1_re * c_im + a1_im * c_re, 0.0)
            for lv in range(n_levels):
                d = 1 << lv
                p_re = pw_ref[2 * lv:2 * lv + 1, s * sw:(s + 1) * sw]
                p_im = pw_ref[2 * lv + 1:2 * lv + 2, s * sw:(s + 1) * sw]
                keep = rows >= d
                s_re = jnp.where(keep, pltpu.roll(h_re, d, 0), 0.0)
                s_im = jnp.where(keep, pltpu.roll(h_im, d, 0), 0.0)
                h_re, h_im = (h_re + p_re * s_re - p_im * s_im,
                              h_im + p_re * s_im + p_im * s_re)
            hre_ref[k * sub:(k + 1) * sub, :] = h_re
            him_ref[k * sub:(k + 1) * sub, :] = h_im
            c_re = h_re[sub - 1:sub, :]
            c_im = h_im[sub - 1:sub, :]
        car_ref[0:1, s * sw:(s + 1) * sw] = c_re
        car_ref[1:2, s * sw:(s + 1) * sw] = c_im
        y_ref[:, s * LANES:(s + 1) * LANES] = (_dot(hre_ref[...], cre_ref[s])
                                               - _dot(him_ref[...], cim_ref[s]))
    y = y_ref[...] + dsk_ref[...] * u_ref[...]
    y = jax.nn.gelu(y)
    gate = jax.nn.sigmoid(_dot(y, wglu_ref[...]) + bglu_ref[...])
    y = y * gate
    y = y * lax.rsqrt(jnp.mean(y * y, axis=-1, keepdims=True) + EPS) * ng_ref[...]
    o_ref[...] = y.astype(o_ref.dtype)


def _s5_mixer(proj, seq, lam_re, lam_im, log_step, b_re, b_im, c_re, c_im, d_skip, w_glu, b_glu,
              norm_g):
    n = proj.shape[0]
    bsz = n // seq
    tt = S5_TILE
    nt = seq // tt
    lam_re = lam_re.astype(F32)
    lam_im = lam_im.astype(F32)
    step = jnp.exp(log_step.astype(F32))[:, None]
    mag = jnp.exp(lam_re * step)
    ab_re = mag * jnp.cos(lam_im * step)
    ab_im = mag * jnp.sin(lam_im * step)
    den = lam_re * lam_re + lam_im * lam_im
    nr = ab_re - 1.0
    f_re = ((nr * lam_re + ab_im * lam_im) / den)[..., None]
    f_im = ((ab_im * lam_re - nr * lam_im) / den)[..., None]
    b_re = b_re.astype(F32)
    b_im = b_im.astype(F32)
    bb_re = f_re * b_re - f_im * b_im
    bb_im = f_re * b_im + f_im * b_re
    n_levels = int(math.log2(S5_SUB))
    pw = []
    for lv in range(n_levels):
        d = float(1 << lv)
        m = jnp.exp(d * lam_re * step)
        pw.append((m * jnp.cos(d * lam_im * step)).reshape(-1))
        pw.append((m * jnp.sin(d * lam_im * step)).reshape(-1))
    n_state = S5_GROUPS * S5_STATE
    pw = jnp.stack(pw + [jnp.zeros((n_state,), F32)] * (16 - len(pw)), axis=0)
    n_slab = S5_GROUPS // S5_SLAB
    eye = jnp.eye(S5_SLAB, dtype=F32)

    def in_slabs(bb):
        t = bb.reshape(n_slab, S5_SLAB, S5_STATE, S5_GROUP_CH)
        w = jnp.einsum("sgni,gh->sgihn", t, eye)
        return w.reshape(n_slab, S5_SLAB * S5_GROUP_CH, S5_SLAB * S5_STATE).astype(BF16)

    def out_slabs(cc):
        t = cc.astype(F32).reshape(n_slab, S5_SLAB, S5_GROUP_CH, S5_STATE)
        w = jnp.einsum("sgin,gh->sgnhi", t, eye)
        return w.reshape(n_slab, S5_SLAB * S5_STATE, S5_SLAB * S5_GROUP_CH).astype(BF16)

    sw = S5_SLAB * S5_STATE
    full = lambda shape: pl.BlockSpec(shape, lambda b, c: (0,) * len(shape))
    return pl.pallas_call(
        _s5_kernel,
        grid=(bsz, nt),
        in_specs=[pl.BlockSpec((tt, D_GRP), lambda b, c: (b * nt + c, 3)),
                  full((n_slab, LANES, sw)), full((n_slab, LANES, sw)),
                  full((n_slab, sw, LANES)), full((n_slab, sw, LANES)),
                  full((16, n_state)), full((1, D_GRP)), full((D_GRP, D_GRP)), full((1, D_GRP)),
                  full((1, D_GRP))],
        out_specs=pl.BlockSpec((tt, D_GRP), lambda b, c: (b * nt + c, 0)),
        out_shape=jax.ShapeDtypeStruct((n, D_GRP), BF16),
        scratch_shapes=[pltpu.VMEM((tt, sw), F32), pltpu.VMEM((tt, sw), F32),
                        pltpu.VMEM((8, n_state), F32), pltpu.VMEM((tt, D_GRP), F32)],
        compiler_params=_cparams(("arbitrary", "arbitrary")),
        name="s5_mixer",
    )(proj, in_slabs(bb_re), in_slabs(bb_im), out_slabs(c_re), out_slabs(c_im), pw,
      d_skip.reshape(1, D_GRP).astype(F32), w_glu.astype(BF16),
      b_glu.reshape(1, D_GRP).astype(F32), norm_g.reshape(1, D_GRP).astype(F32))


def _hgrn_level_matrices(c):
    n_lv = int(math.log2(c))
    sums = np.zeros((n_lv, c, c), np.float32)
    masks = np.zeros((n_lv + 1, c, c), np.float32)
    masks[0] = np.eye(c, dtype=np.float32)
    for lv in range(1, n_lv + 1):
        blk = 1 << lv
        half = blk // 2
        for r in range(c):
            base = (r // blk) * blk
            m = base + half - 1
            if r > m:
                sums[lv - 1, r, m + 1:r + 1] = 1.0
                masks[lv, r, base:base + half] = 1.0
            else:
                sums[lv - 1, r, r + 1:m + 1] = 1.0
    return sums.reshape(n_lv * c, c), masks


def _hgrn_kernel(q_ref, f_ref, i_ref, g_ref, lb_ref, ng_ref, tri_ref, sums_ref, masks_ref,
                 o_ref, state_ref):
    c = HGRN_CHUNK
    n_lv = int(math.log2(c))
    ci = pl.program_id(1)

    @pl.when(ci == 0)
    def _():
        state_ref[...] = jnp.zeros_like(state_ref)

    for h in range(HGRN_HEADS):
        sl = slice(h * LANES, (h + 1) * LANES)
        lb = lb_ref[:, sl]
        ff = f_ref[:, sl]
        qf = _silu(q_ref[:, sl])
        log_lb = jnp.log(jnp.maximum(lb, LB_FLOOR))
        log_f = jnp.logaddexp(log_lb, jnp.log1p(-lb) + _log_sigmoid(ff))
        kk = (1.0 - lb) * jax.nn.sigmoid(-ff)
        vv = i_ref[:, sl]
        hi, mid, lo = _split3(log_f)
        pieces = jnp.concatenate([hi, mid, lo], axis=1)
        bcs3 = jnp.dot(tri_ref[...], pieces, preferred_element_type=F32)
        b_cs = bcs3[:, 0:LANES] + bcs3[:, LANES:2 * LANES] + bcs3[:, 2 * LANES:3 * LANES]
        e3 = jnp.dot(sums_ref[...], pieces, preferred_element_type=F32)
        e_all = e3[:, 0:LANES] + e3[:, LANES:2 * LANES] + e3[:, 2 * LANES:3 * LANES]
        b_end = b_cs[c - 1:c, :]
        st = state_ref[h]
        o = _dot_nt(qf * jnp.exp(b_cs), st)
        k_end = kk * jnp.exp(b_end - b_cs)
        state_ref[h] = st * jnp.exp(b_end) + _dot_tn(vv, k_end)
        att = masks_ref[0] * _dot_nt(qf, kk)
        for lv in range(n_lv):
            w = jnp.exp(jnp.minimum(e_all[lv * c:(lv + 1) * c, :], 0.0))
            att = att + masks_ref[lv + 1] * _dot_nt(qf * w, kk * w)
        o = o + _dot(att, vv)
        o = o * lax.rsqrt(jnp.mean(o * o, axis=-1, keepdims=True) + EPS) * ng_ref[...]
        o_ref[:, sl] = (o * _silu(g_ref[:, sl])).astype(o_ref.dtype)


def _hgrn_mixer(proj, seq, lb, norm_g):
    n = proj.shape[0]
    bsz = n // seq
    c = HGRN_CHUNK
    nc = seq // c
    n_lv = int(math.log2(c))
    sums, masks = _hgrn_level_matrices(c)
    tri = jnp.asarray(np.tril(np.ones((c, c), np.float32)), BF16)
    full = lambda shape: pl.BlockSpec(shape, lambda b, k: (0,) * len(shape))
    blk = lambda j: pl.BlockSpec((c, D_GRP), lambda b, k, j=j: (b * nc + k, j))
    return pl.pallas_call(
        _hgrn_kernel,
        grid=(bsz, nc),
        in_specs=[blk(4), blk(5), blk(6), blk(7), full((1, D_GRP)), full((1, LANES)),
                  full((c, c)), full((n_lv * c, c)), full((n_lv + 1, c, c))],
        out_specs=pl.BlockSpec((c, D_GRP), lambda b, k: (b * nc + k, 0)),
        out_shape=jax.ShapeDtypeStruct((n, D_GRP), BF16),
        scratch_shapes=[pltpu.VMEM((HGRN_HEADS, LANES, LANES), F32)],
        compiler_params=_cparams(("arbitrary", "arbitrary")),
        name="hgrn_mixer",
    )(proj, proj, proj, proj, lb.reshape(1, D_GRP).astype(F32),
      norm_g.reshape(1, LANES).astype(F32), tri, jnp.asarray(sums, BF16), jnp.asarray(masks, F32))


def _sb_block(q, kb, vb, upper, run, valid):
    z = _dot_nt(q, kb)
    ls = _log_sigmoid(z)
    lk = ls - z
    if valid is not None:
        lk = jnp.where(valid, lk, 0.0)
    between = run + _dot_exact_lhs(lk, upper)
    w = jnp.exp(ls + between)
    if valid is not None:
        w = jnp.where(valid, w, 0.0)
    return _dot(w, vb), run + jnp.sum(lk, axis=-1, keepdims=True)


def _sb_kernel(q_ref, k_ref, v_ref, upper_ref, o_ref):
    tq, tk = SB_TQ, SB_TK
    qi = pl.program_id(2)
    q = q_ref[...] * (SB_HEAD_DIM ** -0.5)
    upper = upper_ref[...]
    row = lax.broadcasted_iota(jnp.int32, (tq, tk), 0)
    col = lax.broadcasted_iota(jnp.int32, (tq, tk), 1)
    k0 = pl.multiple_of(qi * tq, tq)
    acc, run = _sb_block(q, k_ref[pl.ds(k0, tk), :], v_ref[pl.ds(k0, tk), :], upper,
                         jnp.zeros((tq, 1), F32), row > col)
    k1 = pl.multiple_of(jnp.maximum(qi - 1, 0) * tk, tk)
    d_acc, run = _sb_block(q, k_ref[pl.ds(k1, tk), :], v_ref[pl.ds(k1, tk), :], upper, run, qi > 0)
    acc = acc + d_acc

    def alive(run):
        return (jnp.max(run) > SB_DEAD_LOG).astype(jnp.int32)

    def cond(carry):
        j, go, _, _ = carry
        return jnp.logical_and(j >= 0, go > 0)

    def body(carry):
        j, _, acc, run = carry
        ks = pl.multiple_of(j * tk, tk)
        d_acc, run = _sb_block(q, k_ref[pl.ds(ks, tk), :], v_ref[pl.ds(ks, tk), :], upper, run,
                               None)
        return j - 1, alive(run), acc + d_acc, run

    _, _, acc, _ = lax.while_loop(cond, body, (qi - 2, alive(run), acc, run))
    o_ref[...] = acc


def _sb_mixer(proj, seq):
    n = proj.shape[0]
    bsz = n // seq
    tq, tk = SB_TQ, SB_TK
    nq = seq // tq
    upper = jnp.asarray(np.triu(np.ones((tk, tk), np.float32), 1).T, BF16)
    hpg = D_GRP // LANES
    return pl.pallas_call(
        _sb_kernel,
        grid=(bsz, SB_HEADS, nq),
        in_specs=[pl.BlockSpec((tq, LANES), lambda b, h, i: (b * nq + i, 8 * hpg + h)),
                  pl.BlockSpec((seq, LANES), lambda b, h, i: (b, 9 * hpg + h)),
                  pl.BlockSpec((seq, LANES), lambda b, h, i: (b, 10 * hpg + h)),
                  pl.BlockSpec((tk, tk), lambda b, h, i: (0, 0))],
        out_specs=pl.BlockSpec((tq, LANES), lambda b, h, i: (b * nq + i, h)),
        out_shape=jax.ShapeDtypeStruct((n, D_GRP), F32),
        compiler_params=_cparams(("arbitrary", "arbitrary", "arbitrary")),
        name="sb_mixer",
    )(proj, proj, proj, upper)


def _outproj_kernel(p0_ref, p1_ref, p2_ref, p3_ref, sbn_ref, w_ref, h_ref, gate_ref, o_ref):
    sb = p3_ref[...]
    sb = sb * lax.rsqrt(jnp.mean(sb * sb, axis=-1, keepdims=True) + EPS) * sbn_ref[...]
    acc = jnp.dot(p0_ref[...], w_ref[0:D_GRP, :], preferred_element_type=F32)
    acc = acc + jnp.dot(p1_ref[...], w_ref[D_GRP:2 * D_GRP, :], preferred_element_type=F32)
    acc = acc + jnp.dot(p2_ref[...], w_ref[2 * D_GRP:3 * D_GRP, :], preferred_element_type=F32)
    acc = acc + jnp.dot(sb.astype(BF16), w_ref[3 * D_GRP:4 * D_GRP, :],
                        preferred_element_type=F32)
    o_ref[...] = h_ref[...] + gate_ref[0] * acc


def _out_proj(y_ssd, y_s5, y_hgrn, o_sb, sb_norm, w_out, h2, gate, seq, tm=512, tn=1024):
    n, d = h2.shape
    bsz = gate.shape[0]
    tpb = seq // tm
    part = lambda: pl.BlockSpec((tm, D_GRP), lambda j, i: (i, 0))
    return pl.pallas_call(
        _outproj_kernel,
        grid=(d // tn, n // tm),
        in_specs=[part(), part(), part(), part(),
                  pl.BlockSpec((1, D_GRP), lambda j, i: (0, 0)),
                  pl.BlockSpec((4 * D_GRP, tn), lambda j, i: (0, j)),
                  pl.BlockSpec((tm, tn), lambda j, i: (i, j)),
                  pl.BlockSpec((1, 1, tn), lambda j, i: (i // tpb, 0, j))],
        out_specs=pl.BlockSpec((tm, tn), lambda j, i: (i, j)),
        out_shape=jax.ShapeDtypeStruct((n, d), F32),
        compiler_params=_cparams(("arbitrary", "arbitrary")),
        name="out_proj",
    )(y_ssd, y_s5, y_hgrn, o_sb, sb_norm.reshape(1, D_GRP).astype(F32), w_out.astype(BF16), h2,
      gate.reshape(bsz, 1, d))


def _router_kernel(x_ref, g_ref, sc_ref, sh_ref, w_ref, bias_ref, wsg_ref, wsu_ref, wsd_ref,
                   pk_ref, idx_ref, gate_ref, ysh_ref):
    x = x_ref[...]
    y = x * lax.rsqrt(jnp.mean(x * x, axis=-1, keepdims=True) + EPS) * g_ref[...]
    y = y * (1.0 + sc_ref[0]) + sh_ref[0]
    tm = y.shape[0]
    packed = _pack_rows(y)
    for s in range(PACK_S):
        pk_ref[pl.ds(s, tm, stride=PACK_S), :] = packed[:, s * LANES:(s + 1) * LANES]
    yb = y.astype(BF16)
    a = (_silu(jnp.dot(yb, wsg_ref[...], preferred_element_type=F32))
         * jnp.dot(yb, wsu_ref[...], preferred_element_type=F32)).astype(BF16)
    ysh_ref[...] = jnp.dot(a, wsd_ref[...], preferred_element_type=F32).astype(ysh_ref.dtype)
    logits = jnp.dot(y, w_ref[...], preferred_element_type=F32, precision=lax.Precision.HIGHEST)
    scores = jax.nn.sigmoid(logits)
    lane = lax.broadcasted_iota(jnp.int32, (tm, LANES), 1)
    sel = jnp.where(lane < N_EXPERTS, scores + bias_ref[...], -jnp.inf)
    idx_out = jnp.zeros((tm, LANES), jnp.int32)
    gate_out = jnp.zeros((tm, LANES), F32)
    total = jnp.zeros((tm, 1), F32)
    for k in range(TOP_K):
        m = jnp.max(sel, axis=-1, keepdims=True)
        am = jnp.min(jnp.where(sel == m, lane, LANES), axis=-1, keepdims=True)
        hit = lane == am
        gk = jnp.sum(jnp.where(hit, scores, 0.0), axis=-1, keepdims=True)
        total = total + gk
        idx_out = jnp.where(lane == k, am, idx_out)
        gate_out = jnp.where(lane == k, gk, gate_out)
        sel = jnp.where(hit, -jnp.inf, sel)
    idx_ref[...] = idx_out
    gate_ref[...] = gate_out / total * ROUTED_SCALE


def _norm_router_shared(h2, g, scale, shift, seq, w_router, e_bias, ws_gate, ws_up, ws_down):
    n, d = h2.shape
    tm = ROUTER_TM
    bsz = scale.shape[0]
    tpb = seq // tm
    ff = ws_gate.shape[-1]
    w = jnp.zeros((d, LANES), F32).at[:, :N_EXPERTS].set(w_router.astype(F32))
    bias = jnp.zeros((1, LANES), F32).at[0, :N_EXPERTS].set(e_bias.astype(F32))
    const = lambda shape: pl.BlockSpec(shape, lambda i: (0,) * len(shape))
    return pl.pallas_call(
        _router_kernel,
        grid=(n // tm,),
        in_specs=[pl.BlockSpec((tm, d), lambda i: (i, 0)),
                  const((1, d)),
                  pl.BlockSpec((1, 1, d), lambda i: (i // tpb, 0, 0)),
                  pl.BlockSpec((1, 1, d), lambda i: (i // tpb, 0, 0)),
                  const((d, LANES)), const((1, LANES)),
                  const((d, ff)), const((d, ff)), const((ff, d))],
        out_specs=[pl.BlockSpec((tm * PACK_S, LANES), lambda i: (i, 0)),
                   pl.BlockSpec((tm, LANES), lambda i: (i, 0)),
                   pl.BlockSpec((tm, LANES), lambda i: (i, 0)),
                   pl.BlockSpec((tm, d), lambda i: (i, 0))],
        out_shape=[jax.ShapeDtypeStruct((n * PACK_S, LANES), jnp.uint32),
                   jax.ShapeDtypeStruct((n, LANES), jnp.int32),
                   jax.ShapeDtypeStruct((n, LANES), F32),
                   jax.ShapeDtypeStruct((n, d), BF16)],
        compiler_params=_cparams(("arbitrary",)),
        name="norm_router_shared",
    )(h2, g.reshape(1, d), scale.reshape(bsz, 1, d), shift.reshape(bsz, 1, d), w, bias,
      ws_gate.astype(BF16), ws_up.astype(BF16), ws_down.astype(BF16))


def _expert_kernel(be_ref, tab_ref, nused_ref, x_hbm, wg_ref, wu_ref, wd_ref, y_hbm,
                   xa, xb, xbf, ya, yb, sem_g, sem_s):
    rows = MOE_ROWS
    b = pl.program_id(0)
    last = nused_ref[0] - 1
    tok_mask = (1 << TOK_BITS) - 1
    kc = 1024

    def gather(blk, buf, sem):
        base = (blk + 1) * rows
        for r in range(rows):
            tok = tab_ref[base + r] & tok_mask
            pltpu.make_async_copy(x_hbm.at[pl.ds(tok * PACK_S, PACK_S), :],
                                  buf.at[pl.ds(r * BUF_PITCH, PACK_S), :], sem).start()

    def scatter(tab_blk, buf, sem):
        base = tab_blk * rows
        for r in range(rows):
            dst = pl.multiple_of((tab_ref[base + r] >> TOK_BITS) * SUBLANES, SUBLANES)
            pltpu.make_async_copy(buf.at[pl.ds(r * BUF_PITCH, PACK_S), :],
                                  y_hbm.at[pl.ds(dst, PACK_S), :], sem).start(priority=1)

    def wait_in(buf, sem):
        pltpu.make_async_copy(x_hbm.at[pl.ds(0, rows * PACK_S), :],
                              buf.at[pl.ds(0, rows * PACK_S), :], sem).wait()

    def wait_out(buf, sem):
        pltpu.make_async_copy(buf.at[pl.ds(0, rows * PACK_S), :],
                              y_hbm.at[pl.ds(0, rows * PACK_S), :], sem).wait()

    @pl.when(b == 0)
    def _():
        yb[...] = jnp.zeros_like(yb)
        gather(0, xa, sem_g.at[0])
        for half in range(2):
            spare = y_hbm.shape[0] - (2 - half) * rows * PACK_S
            fill = pltpu.make_async_copy(yb.at[pl.ds(0, rows * PACK_S), :],
                                         y_hbm.at[pl.ds(spare, rows * PACK_S), :], sem_s.at[0])
            fill.start()
            fill.wait()

    def step(cur_x, nxt_x, cur_y, prv_y, cur, nxt):
        wait_in(cur_x, sem_g.at[cur])

        @pl.when(b >= 1)
        def _():
            wait_out(cur_y, sem_s.at[cur])

        gather(jnp.minimum(b + 1, last), nxt_x, sem_g.at[nxt])
        scatter(b, prv_y, sem_s.at[nxt])
        for s in range(PACK_S):
            lo, hi = _unpack_words(cur_x[pl.ds(s, rows, stride=BUF_PITCH), :])
            xbf[:, s * LANES:(s + 1) * LANES] = lo.astype(BF16)
            xbf[:, PACK_W + s * LANES:PACK_W + (s + 1) * LANES] = hi.astype(BF16)
        g = u = None
        for c in range(xbf.shape[1] // kc):
            x_c = xbf[:, c * kc:(c + 1) * kc]
            g_c = jnp.dot(x_c, wg_ref[0, 0, c * kc:(c + 1) * kc, :].astype(BF16),
                          preferred_element_type=F32)
            u_c = jnp.dot(x_c, wu_ref[0, 0, c * kc:(c + 1) * kc, :].astype(BF16),
                          preferred_element_type=F32)
            g = g_c if g is None else g + g_c
            u = u_c if u is None else u + u_c
        a = (_silu(g) * u).astype(BF16)
        y = jnp.concatenate(
            [jnp.dot(a, wd_ref[0, 0, :, c * kc:(c + 1) * kc].astype(BF16),
                     preferred_element_type=F32) for c in range(xbf.shape[1] // kc)], axis=1)
        packed = _pack_rows(y)
        for s in range(PACK_S):
            cur_y[pl.ds(s, rows, stride=BUF_PITCH), :] = packed[:, s * LANES:(s + 1) * LANES]

        @pl.when(b == last)
        def _():
            scatter(b + 1, cur_y, sem_s.at[cur])
            wait_out(prv_y, sem_s.at[nxt])
            wait_out(cur_y, sem_s.at[cur])
            wait_in(nxt_x, sem_g.at[nxt])

    @pl.when(jnp.logical_and(b % 2 == 0, b <= last))
    def _():
        step(xa, xb, ya, yb, 0, 1)

    @pl.when(jnp.logical_and(b % 2 == 1, b <= last))
    def _():
        step(xb, xa, yb, ya, 1, 0)


def _routed_experts(x_packed, block_expert, table, n_used, n_y_rows, layer, w_gate, w_up, w_down):
    rows = MOE_ROWS
    n_blocks = block_expert.shape[0]
    d, ff = w_gate.shape[-2:]
    buf = lambda: pltpu.VMEM((rows * BUF_PITCH, LANES), jnp.uint32)
    w_idx = lambda b, be, tab, nu: (layer, be[jnp.minimum(b, nu[0] - 1)], 0, 0)
    grid_spec = pltpu.PrefetchScalarGridSpec(
        num_scalar_prefetch=3,
        grid=(n_blocks,),
        in_specs=[pl.BlockSpec(memory_space=pl.ANY),
                  pl.BlockSpec((1, 1, d, ff), w_idx),
                  pl.BlockSpec((1, 1, d, ff), w_idx),
                  pl.BlockSpec((1, 1, ff, d), w_idx)],
        out_specs=pl.BlockSpec(memory_space=pl.ANY),
        scratch_shapes=[buf(), buf(), pltpu.VMEM((rows, d), BF16), buf(), buf(),
                        pltpu.SemaphoreType.DMA((2,)), pltpu.SemaphoreType.DMA((2,))],
    )
    return pl.pallas_call(
        _expert_kernel,
        grid_spec=grid_spec,
        out_shape=jax.ShapeDtypeStruct((n_y_rows, LANES), jnp.uint32),
        compiler_params=_cparams(("arbitrary",), EXPERT_VMEM_LIMIT),
        name="routed_experts",
    )(block_expert, table, n_used, x_packed, w_gate, w_up, w_down)


def _combine_kernel(*refs, out_norm):
    y_refs = refs[:TOP_K]
    gates_ref, ysh_ref, h_ref, g2_ref, ng_ref, o_ref = refs[TOP_K:]
    tm = COMBINE_TM
    gates = gates_ref[...]
    gk = [gates[:, k:k + 1] for k in range(TOP_K)]
    for s in range(PACK_S):
        acc_lo = ysh_ref[:, s * LANES:(s + 1) * LANES].astype(F32)
        acc_hi = ysh_ref[:, PACK_W + s * LANES:PACK_W + (s + 1) * LANES].astype(F32)
        for k in range(TOP_K):
            lo, hi = _unpack_words(y_refs[k][pl.ds(s, tm, stride=PACK_S), :])
            acc_lo = acc_lo + gk[k] * lo
            acc_hi = acc_hi + gk[k] * hi
        lo_sl = slice(s * LANES, (s + 1) * LANES)
        hi_sl = slice(PACK_W + s * LANES, PACK_W + (s + 1) * LANES)
        o_ref[:, lo_sl] = h_ref[:, lo_sl] + g2_ref[0, :, lo_sl] * acc_lo
        o_ref[:, hi_sl] = h_ref[:, hi_sl] + g2_ref[0, :, hi_sl] * acc_hi
    if out_norm:
        v = o_ref[...]
        o_ref[...] = v * lax.rsqrt(jnp.mean(v * v, axis=-1, keepdims=True) + EPS) * ng_ref[...]


def _combine(y_rows, gates, ysh, h2, gate2, seq, out_g):
    n, d = h2.shape
    tm = COMBINE_TM
    bsz = gate2.shape[0]
    tpb = seq // tm
    nt = n // tm
    slot = lambda k: pl.BlockSpec((tm * PACK_S, LANES), lambda i, k=k: (k * nt + i, 0))
    ng = jnp.ones((1, d), F32) if out_g is None else out_g.reshape(1, d).astype(F32)
    return pl.pallas_call(
        functools.partial(_combine_kernel, out_norm=out_g is not None),
        grid=(nt,),
        in_specs=[slot(k) for k in range(TOP_K)] + [
                  pl.BlockSpec((tm, LANES), lambda i: (i, 0)),
                  pl.BlockSpec((tm, d), lambda i: (i, 0)),
                  pl.BlockSpec((tm, d), lambda i: (i, 0)),
                  pl.BlockSpec((1, 1, d), lambda i: (i // tpb, 0, 0)),
                  pl.BlockSpec((1, d), lambda i: (0, 0))],
        out_specs=pl.BlockSpec((tm, d), lambda i: (i, 0)),
        out_shape=jax.ShapeDtypeStruct((n, d), F32),
        compiler_params=_cparams(("arbitrary",)),
        name="moe_combine",
    )(*([y_rows] * TOP_K), gates, ysh, h2, gate2.reshape(bsz, 1, d), ng)


def _routing_tables(idx, n):
    rows = MOE_ROWS
    n_pairs = n * TOP_K
    n_blocks = -(-(n_pairs + N_EXPERTS * (rows - 1)) // rows)
    e_flat = idx.reshape(-1)
    _, order = lax.sort_key_val(e_flat, jnp.arange(n_pairs, dtype=jnp.int32))
    experts = jnp.arange(N_EXPERTS, dtype=jnp.int32)
    counts = jnp.sum((e_flat[:, None] == experts[None, :]).astype(jnp.int32), axis=0)
    start = jnp.cumsum(counts) - counts
    padded = (counts + rows - 1) // rows * rows
    pad_end = jnp.cumsum(padded)
    pad_start = pad_end - padded
    block_start = jnp.arange(n_blocks, dtype=jnp.int32) * rows
    block_expert = jnp.minimum(
        jnp.sum((pad_end[None, :] <= block_start[:, None]).astype(jnp.int32), axis=1), N_EXPERTS - 1)
    onehot = (block_expert[:, None] == experts[None, :]).astype(jnp.int32)
    blk_pad_start = jnp.sum(onehot * pad_start[None, :], axis=1)
    blk_start = jnp.sum(onehot * start[None, :], axis=1)
    blk_count = jnp.sum(onehot * counts[None, :], axis=1)
    r_in = jnp.arange(rows, dtype=jnp.int32)[None, :]
    j = block_start[:, None] + r_in - blk_pad_start[:, None]
    valid = j < blk_count[:, None]
    src = jnp.clip(blk_start[:, None] + j, 0, n_pairs - 1)
    pair = jnp.take(order, src.reshape(-1), axis=0).reshape(n_blocks, rows)
    s8 = PACK_S // SUBLANES
    spare = (n_pairs + (jnp.arange(n_blocks, dtype=jnp.int32)[:, None] % 2) * rows + r_in) * s8
    tok = jnp.where(valid, pair // TOP_K, 0)
    dst = jnp.where(valid, ((pair % TOP_K) * n + pair // TOP_K) * s8, spare)
    table = (dst << TOK_BITS) | tok
    lead = ((n_pairs + rows + r_in) * s8) << TOK_BITS
    table = jnp.concatenate([lead, table], axis=0).reshape(-1).astype(jnp.int32)
    n_used = (pad_end[-1] // rows).astype(jnp.int32).reshape(1)
    return block_expert.astype(jnp.int32), table, n_used, (n_pairs + 2 * rows) * PACK_S


def _moe(h2, norm_g, scale2, shift2, gate2, seq, layer, w_router, e_bias, w_gate, w_up, w_down,
         ws_gate, ws_up, ws_down, out_g):
    n, d = h2.shape
    x_packed, idx, gates, ysh = _norm_router_shared(h2, norm_g, scale2, shift2, seq, w_router,
                                                    e_bias, ws_gate, ws_up, ws_down)
    block_expert, table, n_used, n_y_rows = _routing_tables(idx[:, :TOP_K], n)
    y_rows = _routed_experts(x_packed, block_expert, table, n_used, n_y_rows, layer,
                             w_gate, w_up, w_down)
    return _combine(y_rows, gates, ysh, h2, gate2, seq, out_g)


def _in_proj_weights(w_in_l):
    dt0 = 3 * D_GRP
    main = jnp.concatenate([w_in_l[:, :dt0], w_in_l[:, dt0 + SSD_HEADS:]], axis=1).astype(BF16)
    dtw = jnp.zeros((w_in_l.shape[0], LANES), BF16).at[:, :SSD_HEADS].set(
        w_in_l[:, dt0:dt0 + SSD_HEADS].astype(BF16))
    return main, dtw


def kernel(x, c, w_ada, b_ada, ada_layer, norm1, w_in, ssd_conv_w, ssd_conv_b, ssd_dt_bias, ssd_a_log, ssd_d, ssd_norm, s5_lam_re, s5_lam_im, s5_log_step, s5_b_re, s5_b_im, s5_c_re, s5_c_im, s5_d, s5_w_glu, s5_b_glu, s5_norm, hgrn_lb_logits, hgrn_norm, sb_norm, w_out, norm2, w_router, e_bias, w_gate, w_up, w_down, ws_gate, ws_up, ws_down, final_norm):
    bsz, seq, d = x.shape
    n = bsz * seq
    depth = w_in.shape[0]
    mod = _ada_proj(c, w_ada, b_ada).reshape(bsz, N_MOD, d)
    lb_p = jax.nn.softmax(hgrn_lb_logits.astype(F32), axis=0)
    lower_bounds = jnp.cumsum(lb_p, axis=0) - lb_p[0]
    h = x.reshape(n, d)
    for layer in range(depth):
        shift1, scale1, gate1, shift2, scale2, gate2 = [
            mod[:, j] + ada_layer[layer, j] for j in range(N_MOD)]
        hn = _norm_mod(h, norm1[layer], scale1, shift1, seq, BF16)
        w_main, w_dt = _in_proj_weights(w_in[layer])
        proj = _matmul(hn, w_main, 512, 1024, name="in_proj")
        dt_raw = _matmul(hn, w_dt, 512, LANES, name="in_proj_dt")
        y_ssd = _ssd_mixer(proj, dt_raw, seq, ssd_conv_w[layer], ssd_conv_b[layer],
                           ssd_dt_bias[layer], ssd_a_log[layer], ssd_d[layer], ssd_norm[layer])
        y_s5 = _s5_mixer(proj, seq, s5_lam_re[layer], s5_lam_im[layer], s5_log_step[layer],
                         s5_b_re[layer], s5_b_im[layer], s5_c_re[layer], s5_c_im[layer],
                         s5_d[layer], s5_w_glu[layer], s5_b_glu[layer], s5_norm[layer])
        y_hgrn = _hgrn_mixer(proj, seq, lower_bounds[layer], hgrn_norm[layer])
        o_sb = _sb_mixer(proj, seq)
        h = _out_proj(y_ssd, y_s5, y_hgrn, o_sb, sb_norm[layer], w_out[layer], h, gate1, seq)
        h = _moe(h, norm2[layer], scale2, shift2, gate2, seq, layer, w_router[layer], e_bias[layer],
                 w_gate, w_up, w_down, ws_gate[layer], ws_up[layer], ws_down[layer],
                 final_norm if layer == depth - 1 else None)
    return h.reshape(bsz, seq, d)
```

```python
import functools
import math

import jax
import jax.numpy as jnp
import numpy as np
from jax import lax
from jax.experimental import pallas as pl
from jax.experimental.pallas import tpu as pltpu

F32 = jnp.float32
BF16 = jnp.bfloat16
EPS = 1e-6
LB_FLOOR = 1e-30

D_MODEL = 4096
N_MOD = 6
D_GRP = 1024
SSD_HEADS = 16
SSD_HEAD_DIM = 64
SSD_GROUPS = 4
SSD_STATE = 128
SSD_CONV = 4
S5_GROUPS = 64
S5_GROUP_CH = 16
S5_STATE = 64
HGRN_HEADS = 8
SB_HEADS = 8
SB_HEAD_DIM = 128
N_EXPERTS = 64
TOP_K = 8
EXPERT_FF = 384
ROUTED_SCALE = 2.5

LANES = 128
VMEM_LIMIT = 56 * 1024 * 1024
EXPERT_VMEM_LIMIT = 60 * 1024 * 1024

SSD_CHUNK = 128
S5_TILE = 256
S5_SUB = 32
S5_SLAB = 8
HGRN_CHUNK = 128
SB_TQ = 256
SB_TK = 256
SB_DEAD_LOG = -104.0
MOE_ROWS = 256
COMBINE_TM = 128
ROUTER_TM = 256
PACK_W = D_MODEL // 2
PACK_S = PACK_W // LANES
TOK_BITS = 13
BUF_PITCH = 24
SUBLANES = 8


def _pack_rows(y):
    bits = pltpu.bitcast(y.astype(BF16).astype(F32), jnp.uint32)
    return (bits[:, :PACK_W] >> 16) | (bits[:, PACK_W:] & jnp.uint32(0xFFFF0000))


def _unpack_words(w):
    lo = pltpu.bitcast(w << 16, F32)
    hi = pltpu.bitcast(w & jnp.uint32(0xFFFF0000), F32)
    return lo, hi


def _cparams(sem, vmem=VMEM_LIMIT):
    return pltpu.CompilerParams(dimension_semantics=sem, vmem_limit_bytes=vmem)


def _dot(a, b):
    return jnp.dot(a.astype(BF16), b.astype(BF16), preferred_element_type=F32)


def _dot_nt(a, b):
    return lax.dot_general(a.astype(BF16), b.astype(BF16), (((1,), (1,)), ((), ())),
                           preferred_element_type=F32)


def _dot_tn(a, b):
    return lax.dot_general(a.astype(BF16), b.astype(BF16), (((0,), (0,)), ((), ())),
                           preferred_element_type=F32)


def _split3(x):
    hi = x.astype(BF16)
    r1 = x - hi.astype(F32)
    mid = r1.astype(BF16)
    lo = (r1 - mid.astype(F32)).astype(BF16)
    return hi, mid, lo


def _dot_exact_rhs(m01, x):
    hi, mid, lo = _split3(x)
    return (jnp.dot(m01, hi, preferred_element_type=F32)
            + jnp.dot(m01, mid, preferred_element_type=F32)
            + jnp.dot(m01, lo, preferred_element_type=F32))


def _dot_exact_lhs(x, m01):
    hi, mid, lo = _split3(x)
    return (jnp.dot(hi, m01, preferred_element_type=F32)
            + jnp.dot(mid, m01, preferred_element_type=F32)
            + jnp.dot(lo, m01, preferred_element_type=F32))


def _silu(x):
    return x * jax.nn.sigmoid(x)


def _log_sigmoid(x):
    return jnp.minimum(x, 0.0) - jnp.log1p(jnp.exp(-jnp.abs(x)))


def _ada_kernel(ct_ref, w_ref, b_ref, o_ref):
    d = w_ref.shape[0]
    kc = 512
    bsz = ct_ref.shape[1]
    acts = _silu(ct_ref[...])
    for bi in range(bsz):
        acc = b_ref[...]
        for c in range(d // kc):
            a_col = acts[c * kc:(c + 1) * kc, bi:bi + 1]
            acc = acc + jnp.sum(w_ref[c * kc:(c + 1) * kc, :] * a_col, axis=0, keepdims=True)
        o_ref[bi:bi + 1, :] = acc


def _ada_proj(c, w_ada, b_ada):
    bsz, d = c.shape
    n = w_ada.shape[1]
    tn = 512
    return pl.pallas_call(
        _ada_kernel,
        grid=(n // tn,),
        in_specs=[pl.BlockSpec((d, bsz), lambda j: (0, 0)),
                  pl.BlockSpec((d, tn), lambda j: (0, j)),
                  pl.BlockSpec((1, tn), lambda j: (0, j))],
        out_specs=pl.BlockSpec((bsz, tn), lambda j: (0, j)),
        out_shape=jax.ShapeDtypeStruct((bsz, n), F32),
        compiler_params=_cparams(("arbitrary",)),
        name="ada_proj",
    )(c.T, w_ada, b_ada.reshape(1, n))


def _in_proj_kernel(x_ref, g_ref, sc_ref, sh_ref, w_ref, wdt_ref, o_ref, dt_ref, hn_ref):
    @pl.when(pl.program_id(1) == 0)
    def _():
        x = x_ref[...]
        y = x * lax.rsqrt(jnp.mean(x * x, axis=-1, keepdims=True) + EPS) * g_ref[...]
        hn_ref[...] = (y * (1.0 + sc_ref[0]) + sh_ref[0]).astype(BF16)
        dt_ref[...] = jnp.dot(hn_ref[...], wdt_ref[0], preferred_element_type=F32)

    o_ref[...] = jnp.dot(hn_ref[...], w_ref[0], preferred_element_type=F32)


def _in_proj(h2, g, scale, shift, seq, layer, w_main, w_dt, tm=512, tn=1024):
    n, d = h2.shape
    bsz = scale.shape[0]
    tpb = seq // tm
    cols = w_main.shape[-1]
    return pl.pallas_call(
        _in_proj_kernel,
        grid=(n // tm, cols // tn),
        in_specs=[pl.BlockSpec((tm, d), lambda i, j: (i, 0)),
                  pl.BlockSpec((1, d), lambda i, j: (0, 0)),
                  pl.BlockSpec((1, 1, d), lambda i, j: (i // tpb, 0, 0)),
                  pl.BlockSpec((1, 1, d), lambda i, j: (i // tpb, 0, 0)),
                  pl.BlockSpec((1, d, tn), lambda i, j: (layer, 0, j)),
                  pl.BlockSpec((1, d, LANES), lambda i, j: (layer, 0, 0))],
        out_specs=[pl.BlockSpec((tm, tn), lambda i, j: (i, j)),
                   pl.BlockSpec((tm, LANES), lambda i, j: (i, 0))],
        out_shape=[jax.ShapeDtypeStruct((n, cols), F32),
                   jax.ShapeDtypeStruct((n, LANES), F32)],
        scratch_shapes=[pltpu.VMEM((tm, d), BF16)],
        compiler_params=_cparams(("arbitrary", "arbitrary")),
        name="in_proj",
    )(h2, g.reshape(1, d), scale.reshape(bsz, 1, d), shift.reshape(bsz, 1, d), w_main, w_dt)


def _ssd_kernel(z_ref, xs_ref, bc_ref, dt_ref, cw_ref, cb_ref, dtb_ref, a_ref, dsk_ref, ng_ref,
                tri_ref, expand_ref, o_ref, ext_ref, state_ref):
    tc = SSD_CHUNK
    ci = pl.program_id(1)

    @pl.when(ci == 0)
    def _():
        ext_ref[0:8, :] = jnp.zeros((8, 2 * D_GRP), F32)
        state_ref[...] = jnp.zeros_like(state_ref)

    ext_ref[8:8 + tc, 0:D_GRP] = xs_ref[...]
    ext_ref[8:8 + tc, D_GRP:2 * D_GRP] = bc_ref[...]
    conv = cb_ref[...] + cw_ref[3:4, :] * ext_ref[8:8 + tc, :]
    for j in range(1, SSD_CONV):
        conv = conv + cw_ref[3 - j:4 - j, :] * ext_ref[8 - j:8 - j + tc, :]
    ext_ref[0:8, :] = ext_ref[tc:tc + 8, :]
    xbc = _silu(conv)
    xs = xbc[:, 0:D_GRP]
    gs = SSD_GROUPS * SSD_STATE
    bm = xbc[:, D_GRP:D_GRP + gs]
    cm = xbc[:, D_GRP + gs:D_GRP + 2 * gs]

    dt = jax.nn.softplus(dt_ref[...] + dtb_ref[...])
    dta = dt * a_ref[...]
    a_cs = _dot_exact_rhs(tri_ref[...], dta)
    a_cs_t = a_cs.T
    dt_x = _dot_exact_lhs(dt, expand_ref[...])
    acs_x = _dot_exact_lhs(a_cs, expand_ref[...])
    last_x = acs_x[tc - 1:tc, :]
    xdt = xs * dt_x
    xdt_end = xdt * jnp.exp(last_x - acs_x)
    ea_x = jnp.exp(acs_x)
    chunk_decay_x = jnp.exp(last_x)

    row = lax.broadcasted_iota(jnp.int32, (tc, tc), 0)
    col = lax.broadcasted_iota(jnp.int32, (tc, tc), 1)
    causal = row >= col
    lane = lax.broadcasted_iota(jnp.int32, (tc, LANES), 1)
    heads_per_group = SSD_HEADS // SSD_GROUPS
    gw = heads_per_group * SSD_HEAD_DIM

    y_parts = []
    for g in range(SSD_GROUPS):
        bm_g = bm[:, g * SSD_STATE:(g + 1) * SSD_STATE]
        cm_g = cm[:, g * SSD_STATE:(g + 1) * SSD_STATE]
        cb = _dot_nt(cm_g, bm_g)
        st_g = state_ref[:, g * gw:(g + 1) * gw]
        y_off = ea_x[:, g * gw:(g + 1) * gw] * _dot(cm_g, st_g)
        state_ref[:, g * gw:(g + 1) * gw] = (
            st_g * chunk_decay_x[:, g * gw:(g + 1) * gw]
            + _dot_tn(bm_g, xdt_end[:, g * gw:(g + 1) * gw]))
        for pair in range(heads_per_group // 2):
            lo = g * gw + pair * LANES
            x_pair = xdt[:, lo:lo + LANES].astype(BF16)
            ys = []
            for k in range(2):
                h = g * heads_per_group + pair * 2 + k
                seg = a_cs[:, h:h + 1] - a_cs_t[h:h + 1, :]
                decay = jnp.where(causal, jnp.exp(jnp.where(causal, seg, 0.0)), 0.0)
                ys.append(jnp.dot((cb * decay).astype(BF16), x_pair, preferred_element_type=F32))
            y_pair = jnp.where(lane < SSD_HEAD_DIM, ys[0], ys[1])
            y_parts.append(y_pair + y_off[:, pair * LANES:(pair + 1) * LANES])
    y = jnp.concatenate(y_parts, axis=1) + xs * dsk_ref[...]
    y = y * _silu(z_ref[...])
    y = y * lax.rsqrt(jnp.mean(y * y, axis=-1, keepdims=True) + EPS) * ng_ref[...]
    o_ref[...] = y.astype(o_ref.dtype)


def _ssd_mixer(proj, dt_raw, seq, conv_w, conv_b, dt_bias, a_log, d_skip, norm_g):
    n = proj.shape[0]
    bsz = n // seq
    tc = SSD_CHUNK
    nc = seq // tc
    pad = LANES - SSD_HEADS
    dtb = jnp.pad(dt_bias.astype(F32), (0, pad)).reshape(1, LANES)
    a_neg = jnp.pad(-jnp.exp(a_log.astype(F32)), (0, pad)).reshape(1, LANES)
    dsk = jnp.repeat(d_skip.astype(F32), SSD_HEAD_DIM).reshape(1, D_GRP)
    tri = jnp.asarray(np.tril(np.ones((tc, tc), np.float32)), BF16)
    expand = np.zeros((LANES, D_GRP), np.float32)
    for h in range(SSD_HEADS):
        expand[h, h * SSD_HEAD_DIM:(h + 1) * SSD_HEAD_DIM] = 1.0
    expand = jnp.asarray(expand, BF16)
    cdim = 2 * D_GRP
    full = lambda shape: pl.BlockSpec(shape, lambda b, c: (0,) * len(shape))
    return pl.pallas_call(
        _ssd_kernel,
        grid=(bsz, nc),
        in_specs=[pl.BlockSpec((tc, D_GRP), lambda b, c: (b * nc + c, 0)),
                  pl.BlockSpec((tc, D_GRP), lambda b, c: (b * nc + c, 1)),
                  pl.BlockSpec((tc, D_GRP), lambda b, c: (b * nc + c, 2)),
                  pl.BlockSpec((tc, LANES), lambda b, c: (b * nc + c, 0)),
                  full((SSD_CONV, cdim)), full((1, cdim)), full((1, LANES)), full((1, LANES)),
                  full((1, D_GRP)), full((1, D_GRP)), full((tc, tc)), full((LANES, D_GRP))],
        out_specs=pl.BlockSpec((tc, D_GRP), lambda b, c: (b * nc + c, 0)),
        out_shape=jax.ShapeDtypeStruct((n, D_GRP), BF16),
        scratch_shapes=[pltpu.VMEM((tc + 8, cdim), F32),
                        pltpu.VMEM((SSD_STATE, D_GRP), F32)],
        compiler_params=_cparams(("arbitrary", "arbitrary")),
        name="ssd_mixer",
    )(proj, proj, proj, dt_raw, conv_w.astype(F32), conv_b.reshape(1, cdim).astype(F32),
      dtb, a_neg, dsk, norm_g.reshape(1, D_GRP).astype(F32), tri, expand)


def _s5_kernel(u_ref, bre_ref, bim_ref, cre_ref, cim_ref, pw_ref, dsk_ref, wglu_ref, bglu_ref,
               ng_ref, o_ref, hre_ref, him_ref, car_ref, y_ref):
    tt = S5_TILE
    sub = S5_SUB
    sw = S5_SLAB * S5_STATE
    ci = pl.program_id(1)

    @pl.when(ci == 0)
    def _():
        car_ref[...] = jnp.zeros_like(car_ref)

    rows = lax.broadcasted_iota(jnp.int32, (sub, sw), 0)
    n_levels = int(math.log2(sub))
    for s in range(D_GRP // LANES):
        u_s = u_ref[:, s * LANES:(s + 1) * LANES].astype(BF16)
        hre_ref[...] = jnp.dot(u_s, bre_ref[s], preferred_element_type=F32)
        him_ref[...] = jnp.dot(u_s, bim_ref[s], preferred_element_type=F32)
        c_re = car_ref[0:1, s * sw:(s + 1) * sw]
        c_im = car_ref[1:2, s * sw:(s + 1) * sw]
        a1_re = pw_ref[0:1, s * sw:(s + 1) * sw]
        a1_im = pw_ref[1:2, s * sw:(s + 1) * sw]
        for k in range(tt // sub):
            h_re = hre_ref[k * sub:(k + 1) * sub, :]
            h_im = him_ref[k * sub:(k + 1) * sub, :]
            first = rows == 0
            h_re = h_re + jnp.where(first, a1_re * c_re - a1_im * c_im, 0.0)
            h_im = h_im + jnp.where(first, a1_re * c_im + a1_im * c_re, 0.0)
            for lv in range(n_levels):
                d = 1 << lv
                p_re = pw_ref[2 * lv:2 * lv + 1, s * sw:(s + 1) * sw]
                p_im = pw_ref[2 * lv + 1:2 * lv + 2, s * sw:(s + 1) * sw]
                keep = rows >= d
                s_re = jnp.where(keep, pltpu.roll(h_re, d, 0), 0.0)
                s_im = jnp.where(keep, pltpu.roll(h_im, d, 0), 0.0)
                h_re, h_im = (h_re + p_re * s_re - p_im * s_im,
                              h_im + p_re * s_im + p_im * s_re)
            hre_ref[k * sub:(k + 1) * sub, :] = h_re
            him_ref[k * sub:(k + 1) * sub, :] = h_im
            c_re = h_re[sub - 1:sub, :]
            c_im = h_im[sub - 1:sub, :]
        car_ref[0:1, s * sw:(s + 1) * sw] = c_re
        car_ref[1:2, s * sw:(s + 1) * sw] = c_im
        y_ref[:, s * LANES:(s + 1) * LANES] = (_dot(hre_ref[...], cre_ref[s])
                                               - _dot(him_ref[...], cim_ref[s]))
    y = y_ref[...] + dsk_ref[...] * u_ref[...]
    y = jax.nn.gelu(y)
    gate = jax.nn.sigmoid(_dot(y, wglu_ref[...]) + bglu_ref[...])
    y = y * gate
    y = y * lax.rsqrt(jnp.mean(y * y, axis=-1, keepdims=True) + EPS) * ng_ref[...]
    o_ref[...] = y.astype(o_ref.dtype)


def _s5_mixer(proj, seq, lam_re, lam_im, log_step, b_re, b_im, c_re, c_im, d_skip, w_glu, b_glu,
              norm_g):
    n = proj.shape[0]
    bsz = n // seq
    tt = S5_TILE
    nt = seq // tt
    lam_re = lam_re.astype(F32)
    lam_im = lam_im.astype(F32)
    step = jnp.exp(log_step.astype(F32))[:, None]
    mag = jnp.exp(lam_re * step)
    ab_re = mag * jnp.cos(lam_im * step)
    ab_im = mag * jnp.sin(lam_im * step)
    den = lam_re * lam_re + lam_im * lam_im
    nr = ab_re - 1.0
    f_re = ((nr * lam_re + ab_im * lam_im) / den)[..., None]
    f_im = ((ab_im * lam_re - nr * lam_im) / den)[..., None]
    b_re = b_re.astype(F32)
    b_im = b_im.astype(F32)
    bb_re = f_re * b_re - f_im * b_im
    bb_im = f_re * b_im + f_im * b_re
    n_levels = int(math.log2(S5_SUB))
    pw = []
    for lv in range(n_levels):
        d = float(1 << lv)
        m = jnp.exp(d * lam_re * step)
        pw.append((m * jnp.cos(d * lam_im * step)).reshape(-1))
        pw.append((m * jnp.sin(d * lam_im * step)).reshape(-1))
    n_state = S5_GROUPS * S5_STATE
    pw = jnp.stack(pw + [jnp.zeros((n_state,), F32)] * (16 - len(pw)), axis=0)
    n_slab = S5_GROUPS // S5_SLAB
    eye = jnp.eye(S5_SLAB, dtype=F32)

    def in_slabs(bb):
        t = bb.reshape(n_slab, S5_SLAB, S5_STATE, S5_GROUP_CH)
        w = jnp.einsum("sgni,gh->sgihn", t, eye)
        return w.reshape(n_slab, S5_SLAB * S5_GROUP_CH, S5_SLAB * S5_STATE).astype(BF16)

    def out_slabs(cc):
        t = cc.astype(F32).reshape(n_slab, S5_SLAB, S5_GROUP_CH, S5_STATE)
        w = jnp.einsum("sgin,gh->sgnhi", t, eye)
        return w.reshape(n_slab, S5_SLAB * S5_STATE, S5_SLAB * S5_GROUP_CH).astype(BF16)

    sw = S5_SLAB * S5_STATE
    full = lambda shape: pl.BlockSpec(shape, lambda b, c: (0,) * len(shape))
    return pl.pallas_call(
        _s5_kernel,
        grid=(bsz, nt),
        in_specs=[pl.BlockSpec((tt, D_GRP), lambda b, c: (b * nt + c, 3)),
                  full((n_slab, LANES, sw)), full((n_slab, LANES, sw)),
                  full((n_slab, sw, LANES)), full((n_slab, sw, LANES)),
                  full((16, n_state)), full((1, D_GRP)), full((D_GRP, D_GRP)), full((1, D_GRP)),
                  full((1, D_GRP))],
        out_specs=pl.BlockSpec((tt, D_GRP), lambda b, c: (b * nt + c, 0)),
        out_shape=jax.ShapeDtypeStruct((n, D_GRP), BF16),
        scratch_shapes=[pltpu.VMEM((tt, sw), F32), pltpu.VMEM((tt, sw), F32),
                        pltpu.VMEM((8, n_state), F32), pltpu.VMEM((tt, D_GRP), F32)],
        compiler_params=_cparams(("arbitrary", "arbitrary")),
        name="s5_mixer",
    )(proj, in_slabs(bb_re), in_slabs(bb_im), out_slabs(c_re), out_slabs(c_im), pw,
      d_skip.reshape(1, D_GRP).astype(F32), w_glu.astype(BF16),
      b_glu.reshape(1, D_GRP).astype(F32), norm_g.reshape(1, D_GRP).astype(F32))


def _hgrn_level_matrices(c):
    n_lv = int(math.log2(c))
    sums = np.zeros((n_lv, c, c), np.float32)
    masks = np.zeros((n_lv + 1, c, c), np.float32)
    masks[0] = np.eye(c, dtype=np.float32)
    for lv in range(1, n_lv + 1):
        blk = 1 << lv
        half = blk // 2
        for r in range(c):
            base = (r // blk) * blk
            m = base + half - 1
            if r > m:
                sums[lv - 1, r, m + 1:r + 1] = 1.0
                masks[lv, r, base:base + half] = 1.0
            else:
                sums[lv - 1, r, r + 1:m + 1] = 1.0
    return sums.reshape(n_lv * c, c), masks


def _hgrn_kernel(q_ref, f_ref, i_ref, g_ref, lb_ref, ng_ref, tri_ref, sums_ref, masks_ref,
                 o_ref, state_ref):
    c = HGRN_CHUNK
    n_lv = int(math.log2(c))
    ci = pl.program_id(1)

    @pl.when(ci == 0)
    def _():
        state_ref[...] = jnp.zeros_like(state_ref)

    for h in range(HGRN_HEADS):
        sl = slice(h * LANES, (h + 1) * LANES)
        lb = lb_ref[:, sl]
        ff = f_ref[:, sl]
        qf = _silu(q_ref[:, sl])
        log_lb = jnp.log(jnp.maximum(lb, LB_FLOOR))
        log_f = jnp.logaddexp(log_lb, jnp.log1p(-lb) + _log_sigmoid(ff))
        kk = (1.0 - lb) * jax.nn.sigmoid(-ff)
        vv = i_ref[:, sl]
        hi, mid, lo = _split3(log_f)
        pieces = jnp.concatenate([hi, mid, lo], axis=1)
        bcs3 = jnp.dot(tri_ref[...], pieces, preferred_element_type=F32)
        b_cs = bcs3[:, 0:LANES] + bcs3[:, LANES:2 * LANES] + bcs3[:, 2 * LANES:3 * LANES]
        e3 = jnp.dot(sums_ref[...], pieces, preferred_element_type=F32)
        e_all = e3[:, 0:LANES] + e3[:, LANES:2 * LANES] + e3[:, 2 * LANES:3 * LANES]
        b_end = b_cs[c - 1:c, :]
        st = state_ref[h]
        o = _dot_nt(qf * jnp.exp(b_cs), st)
        k_end = kk * jnp.exp(b_end - b_cs)
        state_ref[h] = st * jnp.exp(b_end) + _dot_tn(vv, k_end)
        att = masks_ref[0] * _dot_nt(qf, kk)
        for lv in range(n_lv):
            w = jnp.exp(jnp.minimum(e_all[lv * c:(lv + 1) * c, :], 0.0))
            att = att + masks_ref[lv + 1] * _dot_nt(qf * w, kk * w)
        o = o + _dot(att, vv)
        o = o * lax.rsqrt(jnp.mean(o * o, axis=-1, keepdims=True) + EPS) * ng_ref[...]
        o_ref[:, sl] = (o * _silu(g_ref[:, sl])).astype(o_ref.dtype)


def _hgrn_mixer(proj, seq, lb, norm_g):
    n = proj.shape[0]
    bsz = n // seq
    c = HGRN_CHUNK
    nc = seq // c
    n_lv = int(math.log2(c))
    sums, masks = _hgrn_level_matrices(c)
    tri = jnp.asarray(np.tril(np.ones((c, c), np.float32)), BF16)
    full = lambda shape: pl.BlockSpec(shape, lambda b, k: (0,) * len(shape))
    blk = lambda j: pl.BlockSpec((c, D_GRP), lambda b, k, j=j: (b * nc + k, j))
    return pl.pallas_call(
        _hgrn_kernel,
        grid=(bsz, nc),
        in_specs=[blk(4), blk(5), blk(6), blk(7), full((1, D_GRP)), full((1, LANES)),
                  full((c, c)), full((n_lv * c, c)), full((n_lv + 1, c, c))],
        out_specs=pl.BlockSpec((c, D_GRP), lambda b, k: (b * nc + k, 0)),
        out_shape=jax.ShapeDtypeStruct((n, D_GRP), BF16),
        scratch_shapes=[pltpu.VMEM((HGRN_HEADS, LANES, LANES), F32)],
        compiler_params=_cparams(("arbitrary", "arbitrary")),
        name="hgrn_mixer",
    )(proj, proj, proj, proj, lb.reshape(1, D_GRP).astype(F32),
      norm_g.reshape(1, LANES).astype(F32), tri, jnp.asarray(sums, BF16), jnp.asarray(masks, F32))


def _sb_block(q, kb, vb, upper, run, valid):
    z = _dot_nt(q, kb)
    ls = _log_sigmoid(z)
    lk = ls - z
    if valid is not None:
        lk = jnp.where(valid, lk, 0.0)
    between = run + _dot_exact_lhs(lk, upper)
    w = jnp.exp(ls + between)
    if valid is not None:
        w = jnp.where(valid, w, 0.0)
    return _dot(w, vb), run + jnp.sum(lk, axis=-1, keepdims=True)


def _sb_kernel(q_ref, k_ref, v_ref, upper_ref, o_ref):
    tq, tk = SB_TQ, SB_TK
    qi = pl.program_id(2)
    q = q_ref[...] * (SB_HEAD_DIM ** -0.5)
    upper = upper_ref[...]
    row = lax.broadcasted_iota(jnp.int32, (tq, tk), 0)
    col = lax.broadcasted_iota(jnp.int32, (tq, tk), 1)
    k0 = pl.multiple_of(qi * tq, tq)
    acc, run = _sb_block(q, k_ref[pl.ds(k0, tk), :], v_ref[pl.ds(k0, tk), :], upper,
                         jnp.zeros((tq, 1), F32), row > col)
    k1 = pl.multiple_of(jnp.maximum(qi - 1, 0) * tk, tk)
    d_acc, run = _sb_block(q, k_ref[pl.ds(k1, tk), :], v_ref[pl.ds(k1, tk), :], upper, run, qi > 0)
    acc = acc + d_acc

    def alive(run):
        return (jnp.max(run) > SB_DEAD_LOG).astype(jnp.int32)

    def cond(carry):
        j, go, _, _ = carry
        return jnp.logical_and(j >= 0, go > 0)

    def body(carry):
        j, _, acc, run = carry
        ks = pl.multiple_of(j * tk, tk)
        d_acc, run = _sb_block(q, k_ref[pl.ds(ks, tk), :], v_ref[pl.ds(ks, tk), :], upper, run,
                               None)
        return j - 1, alive(run), acc + d_acc, run

    _, _, acc, _ = lax.while_loop(cond, body, (qi - 2, alive(run), acc, run))
    o_ref[...] = acc


def _sb_mixer(proj, seq):
    n = proj.shape[0]
    bsz = n // seq
    tq, tk = SB_TQ, SB_TK
    nq = seq // tq
    upper = jnp.asarray(np.triu(np.ones((tk, tk), np.float32), 1).T, BF16)
    hpg = D_GRP // LANES
    return pl.pallas_call(
        _sb_kernel,
        grid=(bsz, SB_HEADS, nq),
        in_specs=[pl.BlockSpec((tq, LANES), lambda b, h, i: (b * nq + i, 8 * hpg + h)),
                  pl.BlockSpec((seq, LANES), lambda b, h, i: (b, 9 * hpg + h)),
                  pl.BlockSpec((seq, LANES), lambda b, h, i: (b, 10 * hpg + h)),
                  pl.BlockSpec((tk, tk), lambda b, h, i: (0, 0))],
        out_specs=pl.BlockSpec((tq, LANES), lambda b, h, i: (b * nq + i, h)),
        out_shape=jax.ShapeDtypeStruct((n, D_GRP), F32),
        compiler_params=_cparams(("arbitrary", "arbitrary", "arbitrary")),
        name="sb_mixer",
    )(proj, proj, proj, upper)


def _outproj_kernel(p0_ref, p1_ref, p2_ref, p3_ref, sbn_ref, w_ref, h_ref, gate_ref, o_ref):
    sb = p3_ref[...]
    sb = sb * lax.rsqrt(jnp.mean(sb * sb, axis=-1, keepdims=True) + EPS) * sbn_ref[...]
    acc = jnp.dot(p0_ref[...], w_ref[0, 0:D_GRP, :], preferred_element_type=F32)
    acc = acc + jnp.dot(p1_ref[...], w_ref[0, D_GRP:2 * D_GRP, :], preferred_element_type=F32)
    acc = acc + jnp.dot(p2_ref[...], w_ref[0, 2 * D_GRP:3 * D_GRP, :], preferred_element_type=F32)
    acc = acc + jnp.dot(sb.astype(BF16), w_ref[0, 3 * D_GRP:4 * D_GRP, :],
                        preferred_element_type=F32)
    o_ref[...] = h_ref[...] + gate_ref[0] * acc


def _out_proj(y_ssd, y_s5, y_hgrn, o_sb, sb_norm, layer, w_out, h2, gate, seq, tm=512, tn=1024):
    n, d = h2.shape
    bsz = gate.shape[0]
    tpb = seq // tm
    part = lambda: pl.BlockSpec((tm, D_GRP), lambda j, i: (i, 0))
    return pl.pallas_call(
        _outproj_kernel,
        grid=(d // tn, n // tm),
        in_specs=[part(), part(), part(), part(),
                  pl.BlockSpec((1, D_GRP), lambda j, i: (0, 0)),
                  pl.BlockSpec((1, 4 * D_GRP, tn), lambda j, i: (layer, 0, j)),
                  pl.BlockSpec((tm, tn), lambda j, i: (i, j)),
                  pl.BlockSpec((1, 1, tn), lambda j, i: (i // tpb, 0, j))],
        out_specs=pl.BlockSpec((tm, tn), lambda j, i: (i, j)),
        out_shape=jax.ShapeDtypeStruct((n, d), F32),
        compiler_params=_cparams(("arbitrary", "arbitrary")),
        name="out_proj",
    )(y_ssd, y_s5, y_hgrn, o_sb, sb_norm.reshape(1, D_GRP).astype(F32), w_out, h2,
      gate.reshape(bsz, 1, d))


def _router_kernel(x_ref, g_ref, sc_ref, sh_ref, w_ref, bias_ref, wsg_ref, wsu_ref, wsd_ref,
                   pk_ref, idx_ref, gate_ref, ysh_ref):
    x = x_ref[...]
    y = x * lax.rsqrt(jnp.mean(x * x, axis=-1, keepdims=True) + EPS) * g_ref[...]
    y = y * (1.0 + sc_ref[0]) + sh_ref[0]
    tm = y.shape[0]
    packed = _pack_rows(y)
    for s in range(PACK_S):
        pk_ref[pl.ds(s, tm, stride=PACK_S), :] = packed[:, s * LANES:(s + 1) * LANES]
    yb = y.astype(BF16)
    a = (_silu(jnp.dot(yb, wsg_ref[...], preferred_element_type=F32))
         * jnp.dot(yb, wsu_ref[...], preferred_element_type=F32)).astype(BF16)
    ysh_ref[...] = jnp.dot(a, wsd_ref[...], preferred_element_type=F32).astype(ysh_ref.dtype)
    y_lo = (y - yb.astype(F32)).astype(BF16)
    both = jnp.dot(yb, w_ref[...], preferred_element_type=F32)
    logits = (both[:, :LANES] + both[:, LANES:]
              + jnp.dot(y_lo, w_ref[:, :LANES], preferred_element_type=F32))
    scores = jax.nn.sigmoid(logits)
    lane = lax.broadcasted_iota(jnp.int32, (tm, LANES), 1)
    sel = jnp.where(lane < N_EXPERTS, scores + bias_ref[...], -jnp.inf)
    idx_out = jnp.zeros((tm, LANES), jnp.int32)
    gate_out = jnp.zeros((tm, LANES), F32)
    total = jnp.zeros((tm, 1), F32)
    for k in range(TOP_K):
        m = jnp.max(sel, axis=-1, keepdims=True)
        am = jnp.min(jnp.where(sel == m, lane, LANES), axis=-1, keepdims=True)
        hit = lane == am
        gk = jnp.sum(jnp.where(hit, scores, 0.0), axis=-1, keepdims=True)
        total = total + gk
        idx_out = jnp.where(lane == k, am, idx_out)
        gate_out = jnp.where(lane == k, gk, gate_out)
        sel = jnp.where(hit, -jnp.inf, sel)
    idx_ref[...] = idx_out
    gate_ref[...] = gate_out / total * ROUTED_SCALE


def _norm_router_shared(h2, g, scale, shift, seq, w_router, e_bias, ws_gate, ws_up, ws_down):
    n, d = h2.shape
    tm = ROUTER_TM
    bsz = scale.shape[0]
    tpb = seq // tm
    ff = ws_gate.shape[-1]
    w = jnp.zeros((d, LANES), F32).at[:, :N_EXPERTS].set(w_router.astype(F32))
    w_hi = w.astype(BF16)
    w = jnp.concatenate([w_hi, (w - w_hi.astype(F32)).astype(BF16)], axis=1)
    bias = jnp.zeros((1, LANES), F32).at[0, :N_EXPERTS].set(e_bias.astype(F32))
    const = lambda shape: pl.BlockSpec(shape, lambda i: (0,) * len(shape))
    return pl.pallas_call(
        _router_kernel,
        grid=(n // tm,),
        in_specs=[pl.BlockSpec((tm, d), lambda i: (i, 0)),
                  const((1, d)),
                  pl.BlockSpec((1, 1, d), lambda i: (i // tpb, 0, 0)),
                  pl.BlockSpec((1, 1, d), lambda i: (i // tpb, 0, 0)),
                  const((d, 2 * LANES)), const((1, LANES)),
                  const((d, ff)), const((d, ff)), const((ff, d))],
        out_specs=[pl.BlockSpec((tm * PACK_S, LANES), lambda i: (i, 0)),
                   pl.BlockSpec((tm, LANES), lambda i: (i, 0)),
                   pl.BlockSpec((tm, LANES), lambda i: (i, 0)),
                   pl.BlockSpec((tm, d), lambda i: (i, 0))],
        out_shape=[jax.ShapeDtypeStruct((n * PACK_S, LANES), jnp.uint32),
                   jax.ShapeDtypeStruct((n, LANES), jnp.int32),
                   jax.ShapeDtypeStruct((n, LANES), F32),
                   jax.ShapeDtypeStruct((n, d), BF16)],
        compiler_params=_cparams(("arbitrary",)),
        name="norm_router_shared",
    )(h2, g.reshape(1, d), scale.reshape(bsz, 1, d), shift.reshape(bsz, 1, d), w, bias,
      ws_gate.astype(BF16), ws_up.astype(BF16), ws_down.astype(BF16))


def _expert_kernel(be_ref, tab_ref, nused_ref, x_hbm, wg_ref, wu_ref, wd_ref, y_hbm,
                   xa, xb, xbf, ya, yb, sem_g, sem_s):
    rows = MOE_ROWS
    b = pl.program_id(0)
    last = nused_ref[0] - 1
    tok_mask = (1 << TOK_BITS) - 1
    kc = 1024

    def gather(blk, buf, sem):
        base = (blk + 1) * rows
        for r in range(rows):
            tok = tab_ref[base + r] & tok_mask
            pltpu.make_async_copy(x_hbm.at[pl.ds(tok * PACK_S, PACK_S), :],
                                  buf.at[pl.ds(r * BUF_PITCH, PACK_S), :], sem).start()

    def scatter(tab_blk, buf, sem):
        base = tab_blk * rows
        for r in range(rows):
            dst = pl.multiple_of((tab_ref[base + r] >> TOK_BITS) * SUBLANES, SUBLANES)
            pltpu.make_async_copy(buf.at[pl.ds(r * BUF_PITCH, PACK_S), :],
                                  y_hbm.at[pl.ds(dst, PACK_S), :], sem).start(priority=1)

    def wait_in(buf, sem):
        pltpu.make_async_copy(x_hbm.at[pl.ds(0, rows * PACK_S), :],
                              buf.at[pl.ds(0, rows * PACK_S), :], sem).wait()

    def wait_out(buf, sem):
        pltpu.make_async_copy(buf.at[pl.ds(0, rows * PACK_S), :],
                              y_hbm.at[pl.ds(0, rows * PACK_S), :], sem).wait()

    @pl.when(b == 0)
    def _():
        yb[...] = jnp.zeros_like(yb)
        gather(0, xa, sem_g.at[0])
        for half in range(2):
            spare = y_hbm.shape[0] - (2 - half) * rows * PACK_S
            fill = pltpu.make_async_copy(yb.at[pl.ds(0, rows * PACK_S), :],
                                         y_hbm.at[pl.ds(spare, rows * PACK_S), :], sem_s.at[0])
            fill.start()
            fill.wait()

    def step(cur_x, nxt_x, cur_y, prv_y, cur, nxt):
        wait_in(cur_x, sem_g.at[cur])

        @pl.when(b >= 1)
        def _():
            wait_out(cur_y, sem_s.at[cur])

        gather(jnp.minimum(b + 1, last), nxt_x, sem_g.at[nxt])
        scatter(b, prv_y, sem_s.at[nxt])
        for s in range(PACK_S):
            lo, hi = _unpack_words(cur_x[pl.ds(s, rows, stride=BUF_PITCH), :])
            xbf[:, s * LANES:(s + 1) * LANES] = lo.astype(BF16)
            xbf[:, PACK_W + s * LANES:PACK_W + (s + 1) * LANES] = hi.astype(BF16)
        g = u = None
        for c in range(xbf.shape[1] // kc):
            x_c = xbf[:, c * kc:(c + 1) * kc]
            g_c = jnp.dot(x_c, wg_ref[0, 0, c * kc:(c + 1) * kc, :].astype(BF16),
                          preferred_element_type=F32)
            u_c = jnp.dot(x_c, wu_ref[0, 0, c * kc:(c + 1) * kc, :].astype(BF16),
                          preferred_element_type=F32)
            g = g_c if g is None else g + g_c
            u = u_c if u is None else u + u_c
        a = (_silu(g) * u).astype(BF16)
        y = jnp.concatenate(
            [jnp.dot(a, wd_ref[0, 0, :, c * kc:(c + 1) * kc].astype(BF16),
                     preferred_element_type=F32) for c in range(xbf.shape[1] // kc)], axis=1)
        packed = _pack_rows(y)
        for s in range(PACK_S):
            cur_y[pl.ds(s, rows, stride=BUF_PITCH), :] = packed[:, s * LANES:(s + 1) * LANES]

        @pl.when(b == last)
        def _():
            scatter(b + 1, cur_y, sem_s.at[cur])
            wait_out(prv_y, sem_s.at[nxt])
            wait_out(cur_y, sem_s.at[cur])
            wait_in(nxt_x, sem_g.at[nxt])

    @pl.when(jnp.logical_and(b % 2 == 0, b <= last))
    def _():
        step(xa, xb, ya, yb, 0, 1)

    @pl.when(jnp.logical_and(b % 2 == 1, b <= last))
    def _():
        step(xb, xa, yb, ya, 1, 0)


def _routed_experts(x_packed, block_expert, table, n_used, n_y_rows, layer, w_gate, w_up, w_down):
    rows = MOE_ROWS
    n_blocks = block_expert.shape[0]
    d, ff = w_gate.shape[-2:]
    buf = lambda: pltpu.VMEM((rows * BUF_PITCH, LANES), jnp.uint32)
    w_idx = lambda b, be, tab, nu: (layer, be[jnp.minimum(b, nu[0] - 1)], 0, 0)
    grid_spec = pltpu.PrefetchScalarGridSpec(
        num_scalar_prefetch=3,
        grid=(n_blocks,),
        in_specs=[pl.BlockSpec(memory_space=pl.ANY),
                  pl.BlockSpec((1, 1, d, ff), w_idx),
                  pl.BlockSpec((1, 1, d, ff), w_idx),
                  pl.BlockSpec((1, 1, ff, d), w_idx)],
        out_specs=pl.BlockSpec(memory_space=pl.ANY),
        scratch_shapes=[buf(), buf(), pltpu.VMEM((rows, d), BF16), buf(), buf(),
                        pltpu.SemaphoreType.DMA((2,)), pltpu.SemaphoreType.DMA((2,))],
    )
    return pl.pallas_call(
        _expert_kernel,
        grid_spec=grid_spec,
        out_shape=jax.ShapeDtypeStruct((n_y_rows, LANES), jnp.uint32),
        compiler_params=_cparams(("arbitrary",), EXPERT_VMEM_LIMIT),
        name="routed_experts",
    )(block_expert, table, n_used, x_packed, w_gate, w_up, w_down)


def _combine_kernel(*refs, out_norm):
    y_refs = refs[:TOP_K]
    gates_ref, ysh_ref, h_ref, g2_ref, ng_ref, o_ref = refs[TOP_K:]
    tm = COMBINE_TM
    gates = gates_ref[...]
    gk = [gates[:, k:k + 1] for k in range(TOP_K)]
    for s in range(PACK_S):
        acc_lo = ysh_ref[:, s * LANES:(s + 1) * LANES].astype(F32)
        acc_hi = ysh_ref[:, PACK_W + s * LANES:PACK_W + (s + 1) * LANES].astype(F32)
        for k in range(TOP_K):
            lo, hi = _unpack_words(y_refs[k][pl.ds(s, tm, stride=PACK_S), :])
            acc_lo = acc_lo + gk[k] * lo
            acc_hi = acc_hi + gk[k] * hi
        lo_sl = slice(s * LANES, (s + 1) * LANES)
        hi_sl = slice(PACK_W + s * LANES, PACK_W + (s + 1) * LANES)
        o_ref[:, lo_sl] = h_ref[:, lo_sl] + g2_ref[0, :, lo_sl] * acc_lo
        o_ref[:, hi_sl] = h_ref[:, hi_sl] + g2_ref[0, :, hi_sl] * acc_hi
    if out_norm:
        v = o_ref[...]
        o_ref[...] = v * lax.rsqrt(jnp.mean(v * v, axis=-1, keepdims=True) + EPS) * ng_ref[...]


def _combine(y_rows, gates, ysh, h2, gate2, seq, out_g):
    n, d = h2.shape
    tm = COMBINE_TM
    bsz = gate2.shape[0]
    tpb = seq // tm
    nt = n // tm
    slot = lambda k: pl.BlockSpec((tm * PACK_S, LANES), lambda i, k=k: (k * nt + i, 0))
    ng = jnp.ones((1, d), F32) if out_g is None else out_g.reshape(1, d).astype(F32)
    return pl.pallas_call(
        functools.partial(_combine_kernel, out_norm=out_g is not None),
        grid=(nt,),
        in_specs=[slot(k) for k in range(TOP_K)] + [
                  pl.BlockSpec((tm, LANES), lambda i: (i, 0)),
                  pl.BlockSpec((tm, d), lambda i: (i, 0)),
                  pl.BlockSpec((tm, d), lambda i: (i, 0)),
                  pl.BlockSpec((1, 1, d), lambda i: (i // tpb, 0, 0)),
                  pl.BlockSpec((1, d), lambda i: (0, 0))],
        out_specs=pl.BlockSpec((tm, d), lambda i: (i, 0)),
        out_shape=jax.ShapeDtypeStruct((n, d), F32),
        compiler_params=_cparams(("arbitrary",)),
        name="moe_combine",
    )(*([y_rows] * TOP_K), gates, ysh, h2, gate2.reshape(bsz, 1, d), ng)


def _routing_tables(idx, n):
    rows = MOE_ROWS
    n_pairs = n * TOP_K
    n_blocks = -(-(n_pairs + N_EXPERTS * (rows - 1)) // rows)
    e_flat = idx.reshape(-1)
    _, order = lax.sort_key_val(e_flat, jnp.arange(n_pairs, dtype=jnp.int32))
    experts = jnp.arange(N_EXPERTS, dtype=jnp.int32)
    counts = jnp.sum((e_flat[:, None] == experts[None, :]).astype(jnp.int32), axis=0)
    start = jnp.cumsum(counts) - counts
    padded = (counts + rows - 1) // rows * rows
    pad_end = jnp.cumsum(padded)
    pad_start = pad_end - padded
    block_start = jnp.arange(n_blocks, dtype=jnp.int32) * rows
    block_expert = jnp.minimum(
        jnp.sum((pad_end[None, :] <= block_start[:, None]).astype(jnp.int32), axis=1), N_EXPERTS - 1)
    onehot = (block_expert[:, None] == experts[None, :]).astype(jnp.int32)
    blk_pad_start = jnp.sum(onehot * pad_start[None, :], axis=1)
    blk_start = jnp.sum(onehot * start[None, :], axis=1)
    blk_count = jnp.sum(onehot * counts[None, :], axis=1)
    r_in = jnp.arange(rows, dtype=jnp.int32)[None, :]
    j = block_start[:, None] + r_in - blk_pad_start[:, None]
    valid = j < blk_count[:, None]
    src = jnp.clip(blk_start[:, None] + j, 0, n_pairs - 1)
    pair = jnp.take(order, src.reshape(-1), axis=0).reshape(n_blocks, rows)
    s8 = PACK_S // SUBLANES
    spare = (n_pairs + (jnp.arange(n_blocks, dtype=jnp.int32)[:, None] % 2) * rows + r_in) * s8
    tok = jnp.where(valid, pair // TOP_K, 0)
    dst = jnp.where(valid, ((pair % TOP_K) * n + pair // TOP_K) * s8, spare)
    table = (dst << TOK_BITS) | tok
    lead = ((n_pairs + rows + r_in) * s8) << TOK_BITS
    table = jnp.concatenate([lead, table], axis=0).reshape(-1).astype(jnp.int32)
    n_used = (pad_end[-1] // rows).astype(jnp.int32).reshape(1)
    return block_expert.astype(jnp.int32), table, n_used, (n_pairs + 2 * rows) * PACK_S


def _moe(h2, norm_g, scale2, shift2, gate2, seq, layer, w_router, e_bias, w_gate, w_up, w_down,
         ws_gate, ws_up, ws_down, out_g):
    n, d = h2.shape
    x_packed, idx, gates, ysh = _norm_router_shared(h2, norm_g, scale2, shift2, seq, w_router,
                                                    e_bias, ws_gate, ws_up, ws_down)
    block_expert, table, n_used, n_y_rows = _routing_tables(idx[:, :TOP_K], n)
    y_rows = _routed_experts(x_packed, block_expert, table, n_used, n_y_rows, layer,
                             w_gate, w_up, w_down)
    return _combine(y_rows, gates, ysh, h2, gate2, seq, out_g)


def _in_proj_weights(w_in):
    dt0 = 3 * D_GRP
    main = jnp.concatenate([w_in[:, :, :dt0], w_in[:, :, dt0 + SSD_HEADS:]], axis=2).astype(BF16)
    dtw = jnp.pad(w_in[:, :, dt0:dt0 + SSD_HEADS].astype(BF16),
                  ((0, 0), (0, 0), (0, LANES - SSD_HEADS)))
    return main, dtw


def kernel(x, c, w_ada, b_ada, ada_layer, norm1, w_in, ssd_conv_w, ssd_conv_b, ssd_dt_bias, ssd_a_log, ssd_d, ssd_norm, s5_lam_re, s5_lam_im, s5_log_step, s5_b_re, s5_b_im, s5_c_re, s5_c_im, s5_d, s5_w_glu, s5_b_glu, s5_norm, hgrn_lb_logits, hgrn_norm, sb_norm, w_out, norm2, w_router, e_bias, w_gate, w_up, w_down, ws_gate, ws_up, ws_down, final_norm):
    bsz, seq, d = x.shape
    n = bsz * seq
    depth = w_in.shape[0]
    mod = _ada_proj(c, w_ada, b_ada).reshape(bsz, N_MOD, d)
    lb_p = jax.nn.softmax(hgrn_lb_logits.astype(F32), axis=0)
    lower_bounds = jnp.cumsum(lb_p, axis=0) - lb_p[0]
    h = x.reshape(n, d)
    w_main, w_dt = _in_proj_weights(w_in)
    w_out_b = w_out.astype(BF16)
    for layer in range(depth):
        shift1, scale1, gate1, shift2, scale2, gate2 = [
            mod[:, j] + ada_layer[layer, j] for j in range(N_MOD)]
        proj, dt_raw = _in_proj(h, norm1[layer], scale1, shift1, seq, layer, w_main, w_dt)
        y_ssd = _ssd_mixer(proj, dt_raw, seq, ssd_conv_w[layer], ssd_conv_b[layer],
                           ssd_dt_bias[layer], ssd_a_log[layer], ssd_d[layer], ssd_norm[layer])
        y_s5 = _s5_mixer(proj, seq, s5_lam_re[layer], s5_lam_im[layer], s5_log_step[layer],
                         s5_b_re[layer], s5_b_im[layer], s5_c_re[layer], s5_c_im[layer],
                         s5_d[layer], s5_w_glu[layer], s5_b_glu[layer], s5_norm[layer])
        y_hgrn = _hgrn_mixer(proj, seq, lower_bounds[layer], hgrn_norm[layer])
        o_sb = _sb_mixer(proj, seq)
        h = _out_proj(y_ssd, y_s5, y_hgrn, o_sb, sb_norm[layer], layer, w_out_b, h, gate1, seq)
        h = _moe(h, norm2[layer], scale2, shift2, gate2, seq, layer, w_router[layer], e_bias[layer],
                 w_gate, w_up, w_down, ws_gate[layer], ws_up[layer], ws_down[layer],
                 final_norm if layer == depth - 1 else None)
    return h.reshape(bsz, seq, d)
```

```python
import functools
import math

import jax
import jax.numpy as jnp
import numpy as np
from jax import lax
from jax.experimental import pallas as pl
from jax.experimental.pallas import tpu as pltpu

F32 = jnp.float32
BF16 = jnp.bfloat16
EPS = 1e-6
LB_FLOOR = 1e-30

D_MODEL = 4096
N_MOD = 6
D_GRP = 1024
SSD_HEADS = 16
SSD_HEAD_DIM = 64
SSD_GROUPS = 4
SSD_STATE = 128
SSD_CONV = 4
S5_GROUPS = 64
S5_GROUP_CH = 16
S5_STATE = 64
HGRN_HEADS = 8
SB_HEADS = 8
SB_HEAD_DIM = 128
N_EXPERTS = 64
TOP_K = 8
EXPERT_FF = 384
ROUTED_SCALE = 2.5

LANES = 128
VMEM_LIMIT = 56 * 1024 * 1024
EXPERT_VMEM_LIMIT = 60 * 1024 * 1024

SSD_CHUNK = 128
S5_TILE = 256
S5_SUB = 32
S5_SLAB = 8
HGRN_CHUNK = 128
SB_TQ = 256
SB_TK = 256
SB_DEAD_LOG = -104.0
MOE_ROWS = 256
COMBINE_TM = 128
ROUTER_TM = 256
PACK_W = D_MODEL // 2
PACK_S = PACK_W // LANES
TOK_BITS = 13
BUF_PITCH = 24
SUBLANES = 8


def _pack_rows(y):
    bits = pltpu.bitcast(y.astype(BF16).astype(F32), jnp.uint32)
    return (bits[:, :PACK_W] >> 16) | (bits[:, PACK_W:] & jnp.uint32(0xFFFF0000))


def _unpack_words(w):
    lo = pltpu.bitcast(w << 16, F32)
    hi = pltpu.bitcast(w & jnp.uint32(0xFFFF0000), F32)
    return lo, hi


def _cparams(sem, vmem=VMEM_LIMIT):
    return pltpu.CompilerParams(dimension_semantics=sem, vmem_limit_bytes=vmem)


def _dot(a, b):
    return jnp.dot(a.astype(BF16), b.astype(BF16), preferred_element_type=F32)


def _dot_nt(a, b):
    return lax.dot_general(a.astype(BF16), b.astype(BF16), (((1,), (1,)), ((), ())),
                           preferred_element_type=F32)


def _dot_tn(a, b):
    return lax.dot_general(a.astype(BF16), b.astype(BF16), (((0,), (0,)), ((), ())),
                           preferred_element_type=F32)


def _split3(x):
    hi = x.astype(BF16)
    r1 = x - hi.astype(F32)
    mid = r1.astype(BF16)
    lo = (r1 - mid.astype(F32)).astype(BF16)
    return hi, mid, lo


def _dot_exact_rhs(m01, x):
    hi, mid, lo = _split3(x)
    return (jnp.dot(m01, hi, preferred_element_type=F32)
            + jnp.dot(m01, mid, preferred_element_type=F32)
            + jnp.dot(m01, lo, preferred_element_type=F32))


def _dot_exact_lhs(x, m01):
    hi, mid, lo = _split3(x)
    return (jnp.dot(hi, m01, preferred_element_type=F32)
            + jnp.dot(mid, m01, preferred_element_type=F32)
            + jnp.dot(lo, m01, preferred_element_type=F32))


def _silu(x):
    return x * jax.nn.sigmoid(x)


def _log_sigmoid(x):
    return jnp.minimum(x, 0.0) - jnp.log1p(jnp.exp(-jnp.abs(x)))


def _ada_kernel(ct_ref, w_ref, b_ref, o_ref):
    d = w_ref.shape[0]
    kc = 512
    bsz = ct_ref.shape[1]
    acts = _silu(ct_ref[...])
    for bi in range(bsz):
        acc = b_ref[...]
        for c in range(d // kc):
            a_col = acts[c * kc:(c + 1) * kc, bi:bi + 1]
            acc = acc + jnp.sum(w_ref[c * kc:(c + 1) * kc, :] * a_col, axis=0, keepdims=True)
        o_ref[bi:bi + 1, :] = acc


def _ada_proj(c, w_ada, b_ada):
    bsz, d = c.shape
    n = w_ada.shape[1]
    tn = 1024
    return pl.pallas_call(
        _ada_kernel,
        grid=(n // tn,),
        in_specs=[pl.BlockSpec((d, bsz), lambda j: (0, 0)),
                  pl.BlockSpec((d, tn), lambda j: (0, j)),
                  pl.BlockSpec((1, tn), lambda j: (0, j))],
        out_specs=pl.BlockSpec((bsz, tn), lambda j: (0, j)),
        out_shape=jax.ShapeDtypeStruct((bsz, n), F32),
        compiler_params=_cparams(("arbitrary",)),
        name="ada_proj",
    )(c.T, w_ada, b_ada.reshape(1, n))


def _in_proj_kernel(x_ref, g_ref, sc_ref, sh_ref, w_ref, wdt_ref, o_ref, dt_ref, hn_ref):
    @pl.when(pl.program_id(1) == 0)
    def _():
        x = x_ref[...]
        y = x * lax.rsqrt(jnp.mean(x * x, axis=-1, keepdims=True) + EPS) * g_ref[...]
        hn_ref[...] = (y * (1.0 + sc_ref[0]) + sh_ref[0]).astype(BF16)
        dt_ref[...] = jnp.dot(hn_ref[...], wdt_ref[0], preferred_element_type=F32)

    o_ref[...] = jnp.dot(hn_ref[...], w_ref[0], preferred_element_type=F32)


def _in_proj(h2, g, scale, shift, seq, layer, w_main, w_dt, tm=512, tn=1024):
    n, d = h2.shape
    bsz = scale.shape[0]
    tpb = seq // tm
    cols = w_main.shape[-1]
    return pl.pallas_call(
        _in_proj_kernel,
        grid=(n // tm, cols // tn),
        in_specs=[pl.BlockSpec((tm, d), lambda i, j: (i, 0)),
                  pl.BlockSpec((1, d), lambda i, j: (0, 0)),
                  pl.BlockSpec((1, 1, d), lambda i, j: (i // tpb, 0, 0)),
                  pl.BlockSpec((1, 1, d), lambda i, j: (i // tpb, 0, 0)),
                  pl.BlockSpec((1, d, tn), lambda i, j: (layer, 0, j)),
                  pl.BlockSpec((1, d, LANES), lambda i, j: (layer, 0, 0))],
        out_specs=[pl.BlockSpec((tm, tn), lambda i, j: (i, j)),
                   pl.BlockSpec((tm, LANES), lambda i, j: (i, 0))],
        out_shape=[jax.ShapeDtypeStruct((n, cols), F32),
                   jax.ShapeDtypeStruct((n, LANES), F32)],
        scratch_shapes=[pltpu.VMEM((tm, d), BF16)],
        compiler_params=_cparams(("arbitrary", "arbitrary")),
        name="in_proj",
    )(h2, g.reshape(1, d), scale.reshape(bsz, 1, d), shift.reshape(bsz, 1, d), w_main, w_dt)


def _ssd_kernel(z_ref, xs_ref, bc_ref, dt_ref, cw_ref, cb_ref, dtb_ref, a_ref, dsk_ref, ng_ref,
                tri_ref, expand_ref, o_ref, ext_ref, state_ref):
    tc = SSD_CHUNK
    ci = pl.program_id(1)

    @pl.when(ci == 0)
    def _():
        ext_ref[0:8, :] = jnp.zeros((8, 2 * D_GRP), F32)
        state_ref[...] = jnp.zeros_like(state_ref)

    ext_ref[8:8 + tc, 0:D_GRP] = xs_ref[...]
    ext_ref[8:8 + tc, D_GRP:2 * D_GRP] = bc_ref[...]
    conv = cb_ref[...] + cw_ref[3:4, :] * ext_ref[8:8 + tc, :]
    for j in range(1, SSD_CONV):
        conv = conv + cw_ref[3 - j:4 - j, :] * ext_ref[8 - j:8 - j + tc, :]
    ext_ref[0:8, :] = ext_ref[tc:tc + 8, :]
    xbc = _silu(conv)
    xs = xbc[:, 0:D_GRP]
    gs = SSD_GROUPS * SSD_STATE
    bm = xbc[:, D_GRP:D_GRP + gs]
    cm = xbc[:, D_GRP + gs:D_GRP + 2 * gs]

    dt = jax.nn.softplus(dt_ref[...] + dtb_ref[...])
    dta = dt * a_ref[...]
    a_cs = _dot_exact_rhs(tri_ref[...], dta)
    a_cs_t = a_cs.T
    dt_x = _dot_exact_lhs(dt, expand_ref[...])
    acs_x = _dot_exact_lhs(a_cs, expand_ref[...])
    last_x = acs_x[tc - 1:tc, :]
    xdt = xs * dt_x
    xdt_end = xdt * jnp.exp(last_x - acs_x)
    ea_x = jnp.exp(acs_x)
    chunk_decay_x = jnp.exp(last_x)

    row = lax.broadcasted_iota(jnp.int32, (tc, tc), 0)
    col = lax.broadcasted_iota(jnp.int32, (tc, tc), 1)
    causal = row >= col
    lane = lax.broadcasted_iota(jnp.int32, (tc, LANES), 1)
    heads_per_group = SSD_HEADS // SSD_GROUPS
    gw = heads_per_group * SSD_HEAD_DIM

    y_parts = []
    for g in range(SSD_GROUPS):
        bm_g = bm[:, g * SSD_STATE:(g + 1) * SSD_STATE]
        cm_g = cm[:, g * SSD_STATE:(g + 1) * SSD_STATE]
        cb = _dot_nt(cm_g, bm_g)
        st_g = state_ref[:, g * gw:(g + 1) * gw]
        y_off = ea_x[:, g * gw:(g + 1) * gw] * _dot(cm_g, st_g)
        state_ref[:, g * gw:(g + 1) * gw] = (
            st_g * chunk_decay_x[:, g * gw:(g + 1) * gw]
            + _dot_tn(bm_g, xdt_end[:, g * gw:(g + 1) * gw]))
        for pair in range(heads_per_group // 2):
            lo = g * gw + pair * LANES
            x_pair = xdt[:, lo:lo + LANES].astype(BF16)
            ys = []
            for k in range(2):
                h = g * heads_per_group + pair * 2 + k
                seg = a_cs[:, h:h + 1] - a_cs_t[h:h + 1, :]
                decay = jnp.where(causal, jnp.exp(jnp.where(causal, seg, 0.0)), 0.0)
                ys.append(jnp.dot((cb * decay).astype(BF16), x_pair, preferred_element_type=F32))
            y_pair = jnp.where(lane < SSD_HEAD_DIM, ys[0], ys[1])
            y_parts.append(y_pair + y_off[:, pair * LANES:(pair + 1) * LANES])
    y = jnp.concatenate(y_parts, axis=1) + xs * dsk_ref[...]
    y = y * _silu(z_ref[...])
    y = y * lax.rsqrt(jnp.mean(y * y, axis=-1, keepdims=True) + EPS) * ng_ref[...]
    o_ref[...] = y.astype(o_ref.dtype)


def _ssd_mixer(proj, dt_raw, seq, conv_w, conv_b, dt_bias, a_log, d_skip, norm_g):
    n = proj.shape[0]
    bsz = n // seq
    tc = SSD_CHUNK
    nc = seq // tc
    pad = LANES - SSD_HEADS
    dtb = jnp.pad(dt_bias.astype(F32), (0, pad)).reshape(1, LANES)
    a_neg = jnp.pad(-jnp.exp(a_log.astype(F32)), (0, pad)).reshape(1, LANES)
    dsk = jnp.repeat(d_skip.astype(F32), SSD_HEAD_DIM).reshape(1, D_GRP)
    tri = jnp.asarray(np.tril(np.ones((tc, tc), np.float32)), BF16)
    expand = np.zeros((LANES, D_GRP), np.float32)
    for h in range(SSD_HEADS):
        expand[h, h * SSD_HEAD_DIM:(h + 1) * SSD_HEAD_DIM] = 1.0
    expand = jnp.asarray(expand, BF16)
    cdim = 2 * D_GRP
    full = lambda shape: pl.BlockSpec(shape, lambda b, c: (0,) * len(shape))
    return pl.pallas_call(
        _ssd_kernel,
        grid=(bsz, nc),
        in_specs=[pl.BlockSpec((tc, D_GRP), lambda b, c: (b * nc + c, 0)),
                  pl.BlockSpec((tc, D_GRP), lambda b, c: (b * nc + c, 1)),
                  pl.BlockSpec((tc, D_GRP), lambda b, c: (b * nc + c, 2)),
                  pl.BlockSpec((tc, LANES), lambda b, c: (b * nc + c, 0)),
                  full((SSD_CONV, cdim)), full((1, cdim)), full((1, LANES)), full((1, LANES)),
                  full((1, D_GRP)), full((1, D_GRP)), full((tc, tc)), full((LANES, D_GRP))],
        out_specs=pl.BlockSpec((tc, D_GRP), lambda b, c: (b * nc + c, 0)),
        out_shape=jax.ShapeDtypeStruct((n, D_GRP), BF16),
        scratch_shapes=[pltpu.VMEM((tc + 8, cdim), F32),
                        pltpu.VMEM((SSD_STATE, D_GRP), F32)],
        compiler_params=_cparams(("arbitrary", "arbitrary")),
        name="ssd_mixer",
    )(proj, proj, proj, dt_raw, conv_w.astype(F32), conv_b.reshape(1, cdim).astype(F32),
      dtb, a_neg, dsk, norm_g.reshape(1, D_GRP).astype(F32), tri, expand)


def _s5_kernel(u_ref, bre_ref, bim_ref, cre_ref, cim_ref, pw_ref, dsk_ref, wglu_ref, bglu_ref,
               ng_ref, o_ref, hre_ref, him_ref, car_ref, y_ref):
    tt = S5_TILE
    sub = S5_SUB
    sw = S5_SLAB * S5_STATE
    ci = pl.program_id(1)

    @pl.when(ci == 0)
    def _():
        car_ref[...] = jnp.zeros_like(car_ref)

    rows = lax.broadcasted_iota(jnp.int32, (sub, sw), 0)
    n_levels = int(math.log2(sub))
    for s in range(D_GRP // LANES):
        u_s = u_ref[:, s * LANES:(s + 1) * LANES].astype(BF16)
        hre_ref[...] = jnp.dot(u_s, bre_ref[s], preferred_element_type=F32)
        him_ref[...] = jnp.dot(u_s, bim_ref[s], preferred_element_type=F32)
        c_re = car_ref[0:1, s * sw:(s + 1) * sw]
        c_im = car_ref[1:2, s * sw:(s + 1) * sw]
        a1_re = pw_ref[0:1, s * sw:(s + 1) * sw]
        a1_im = pw_ref[1:2, s * sw:(s + 1) * sw]
        for k in range(tt // sub):
            h_re = hre_ref[k * sub:(k + 1) * sub, :]
            h_im = him_ref[k * sub:(k + 1) * sub, :]
            first = rows == 0
            h_re = h_re + jnp.where(first, a1_re * c_re - a1_im * c_im, 0.0)
            h_im = h_im + jnp.where(first, a1_re * c_im + a1_im * c_re, 0.0)
            for lv in range(n_levels):
                d = 1 << lv
                p_re = pw_ref[2 * lv:2 * lv + 1, s * sw:(s + 1) * sw]
                p_im = pw_ref[2 * lv + 1:2 * lv + 2, s * sw:(s + 1) * sw]
                if d % SUBLANES:
                    keep = rows >= d
                    s_re = jnp.where(keep, pltpu.roll(h_re, d, 0), 0.0)
                    s_im = jnp.where(keep, pltpu.roll(h_im, d, 0), 0.0)
                    h_re, h_im = (h_re + p_re * s_re - p_im * s_im,
                                  h_im + p_re * s_im + p_im * s_re)
                else:
                    s_re, s_im = h_re[:sub - d], h_im[:sub - d]
                    h_re, h_im = (
                        jnp.concatenate([h_re[:d], h_re[d:] + p_re * s_re - p_im * s_im], axis=0),
                        jnp.concatenate([h_im[:d], h_im[d:] + p_re * s_im + p_im * s_re], axis=0))
            hre_ref[k * sub:(k + 1) * sub, :] = h_re
            him_ref[k * sub:(k + 1) * sub, :] = h_im
            c_re = h_re[sub - 1:sub, :]
            c_im = h_im[sub - 1:sub, :]
        car_ref[0:1, s * sw:(s + 1) * sw] = c_re
        car_ref[1:2, s * sw:(s + 1) * sw] = c_im
        y_ref[:, s * LANES:(s + 1) * LANES] = (_dot(hre_ref[...], cre_ref[s])
                                               - _dot(him_ref[...], cim_ref[s]))
    y = y_ref[...] + dsk_ref[...] * u_ref[...]
    y = jax.nn.gelu(y)
    gate = jax.nn.sigmoid(_dot(y, wglu_ref[...]) + bglu_ref[...])
    y = y * gate
    y = y * lax.rsqrt(jnp.mean(y * y, axis=-1, keepdims=True) + EPS) * ng_ref[...]
    o_ref[...] = y.astype(o_ref.dtype)


def _s5_mixer(proj, seq, lam_re, lam_im, log_step, b_re, b_im, c_re, c_im, d_skip, w_glu, b_glu,
              norm_g):
    n = proj.shape[0]
    bsz = n // seq
    tt = S5_TILE
    nt = seq // tt
    lam_re = lam_re.astype(F32)
    lam_im = lam_im.astype(F32)
    step = jnp.exp(log_step.astype(F32))[:, None]
    mag = jnp.exp(lam_re * step)
    ab_re = mag * jnp.cos(lam_im * step)
    ab_im = mag * jnp.sin(lam_im * step)
    den = lam_re * lam_re + lam_im * lam_im
    nr = ab_re - 1.0
    f_re = ((nr * lam_re + ab_im * lam_im) / den)[..., None]
    f_im = ((ab_im * lam_re - nr * lam_im) / den)[..., None]
    b_re = b_re.astype(F32)
    b_im = b_im.astype(F32)
    bb_re = f_re * b_re - f_im * b_im
    bb_im = f_re * b_im + f_im * b_re
    n_levels = int(math.log2(S5_SUB))
    pw = []
    for lv in range(n_levels):
        d = float(1 << lv)
        m = jnp.exp(d * lam_re * step)
        pw.append((m * jnp.cos(d * lam_im * step)).reshape(-1))
        pw.append((m * jnp.sin(d * lam_im * step)).reshape(-1))
    n_state = S5_GROUPS * S5_STATE
    pw = jnp.stack(pw + [jnp.zeros((n_state,), F32)] * (16 - len(pw)), axis=0)
    n_slab = S5_GROUPS // S5_SLAB
    eye = jnp.eye(S5_SLAB, dtype=F32)

    def in_slabs(bb):
        t = bb.reshape(n_slab, S5_SLAB, S5_STATE, S5_GROUP_CH)
        w = jnp.einsum("sgni,gh->sgihn", t, eye)
        return w.reshape(n_slab, S5_SLAB * S5_GROUP_CH, S5_SLAB * S5_STATE).astype(BF16)

    def out_slabs(cc):
        t = cc.astype(F32).reshape(n_slab, S5_SLAB, S5_GROUP_CH, S5_STATE)
        w = jnp.einsum("sgin,gh->sgnhi", t, eye)
        return w.reshape(n_slab, S5_SLAB * S5_STATE, S5_SLAB * S5_GROUP_CH).astype(BF16)

    sw = S5_SLAB * S5_STATE
    full = lambda shape: pl.BlockSpec(shape, lambda b, c: (0,) * len(shape))
    return pl.pallas_call(
        _s5_kernel,
        grid=(bsz, nt),
        in_specs=[pl.BlockSpec((tt, D_GRP), lambda b, c: (b * nt + c, 3)),
                  full((n_slab, LANES, sw)), full((n_slab, LANES, sw)),
                  full((n_slab, sw, LANES)), full((n_slab, sw, LANES)),
                  full((16, n_state)), full((1, D_GRP)), full((D_GRP, D_GRP)), full((1, D_GRP)),
                  full((1, D_GRP))],
        out_specs=pl.BlockSpec((tt, D_GRP), lambda b, c: (b * nt + c, 0)),
        out_shape=jax.ShapeDtypeStruct((n, D_GRP), BF16),
        scratch_shapes=[pltpu.VMEM((tt, sw), F32), pltpu.VMEM((tt, sw), F32),
                        pltpu.VMEM((8, n_state), F32), pltpu.VMEM((tt, D_GRP), F32)],
        compiler_params=_cparams(("arbitrary", "arbitrary")),
        name="s5_mixer",
    )(proj, in_slabs(bb_re), in_slabs(bb_im), out_slabs(c_re), out_slabs(c_im), pw,
      d_skip.reshape(1, D_GRP).astype(F32), w_glu.astype(BF16),
      b_glu.reshape(1, D_GRP).astype(F32), norm_g.reshape(1, D_GRP).astype(F32))


def _hgrn_level_matrices(c):
    n_lv = int(math.log2(c))
    sums = np.zeros((n_lv, c, c), np.float32)
    masks = np.zeros((n_lv + 1, c, c), np.float32)
    masks[0] = np.eye(c, dtype=np.float32)
    for lv in range(1, n_lv + 1):
        blk = 1 << lv
        half = blk // 2
        for r in range(c):
            base = (r // blk) * blk
            m = base + half - 1
            if r > m:
                sums[lv - 1, r, m + 1:r + 1] = 1.0
                masks[lv, r, base:base + half] = 1.0
            else:
                sums[lv - 1, r, r + 1:m + 1] = 1.0
    return sums.reshape(n_lv * c, c), masks


def _hgrn_kernel(q_ref, f_ref, i_ref, g_ref, lb_ref, ng_ref, tri_ref, sums_ref, masks_ref,
                 o_ref, state_ref):
    c = HGRN_CHUNK
    n_lv = int(math.log2(c))
    ci = pl.program_id(1)

    @pl.when(ci == 0)
    def _():
        state_ref[...] = jnp.zeros_like(state_ref)

    for h in range(HGRN_HEADS):
        sl = slice(h * LANES, (h + 1) * LANES)
        lb = lb_ref[:, sl]
        ff = f_ref[:, sl]
        qf = _silu(q_ref[:, sl])
        log_lb = jnp.log(jnp.maximum(lb, LB_FLOOR))
        log_f = jnp.logaddexp(log_lb, jnp.log1p(-lb) + _log_sigmoid(ff))
        kk = (1.0 - lb) * jax.nn.sigmoid(-ff)
        vv = i_ref[:, sl]
        hi, mid, lo = _split3(log_f)
        pieces = jnp.concatenate([hi, mid, lo], axis=1)
        bcs3 = jnp.dot(tri_ref[...], pieces, preferred_element_type=F32)
        b_cs = bcs3[:, 0:LANES] + bcs3[:, LANES:2 * LANES] + bcs3[:, 2 * LANES:3 * LANES]
        e3 = jnp.dot(sums_ref[...], pieces, preferred_element_type=F32)
        e_all = e3[:, 0:LANES] + e3[:, LANES:2 * LANES] + e3[:, 2 * LANES:3 * LANES]
        b_end = b_cs[c - 1:c, :]
        st = state_ref[h]
        o = _dot_nt(qf * jnp.exp(b_cs), st)
        k_end = kk * jnp.exp(b_end - b_cs)
        state_ref[h] = st * jnp.exp(b_end) + _dot_tn(vv, k_end)
        att = masks_ref[0] * _dot_nt(qf, kk)
        for lv in range(n_lv):
            w = jnp.exp(jnp.minimum(e_all[lv * c:(lv + 1) * c, :], 0.0))
            att = att + masks_ref[lv + 1] * _dot_nt(qf * w, kk * w)
        o = o + _dot(att, vv)
        o = o * lax.rsqrt(jnp.mean(o * o, axis=-1, keepdims=True) + EPS) * ng_ref[...]
        o_ref[:, sl] = (o * _silu(g_ref[:, sl])).astype(o_ref.dtype)


def _hgrn_mixer(proj, seq, lb, norm_g):
    n = proj.shape[0]
    bsz = n // seq
    c = HGRN_CHUNK
    nc = seq // c
    n_lv = int(math.log2(c))
    sums, masks = _hgrn_level_matrices(c)
    tri = jnp.asarray(np.tril(np.ones((c, c), np.float32)), BF16)
    full = lambda shape: pl.BlockSpec(shape, lambda b, k: (0,) * len(shape))
    blk = lambda j: pl.BlockSpec((c, D_GRP), lambda b, k, j=j: (b * nc + k, j))
    return pl.pallas_call(
        _hgrn_kernel,
        grid=(bsz, nc),
        in_specs=[blk(4), blk(5), blk(6), blk(7), full((1, D_GRP)), full((1, LANES)),
                  full((c, c)), full((n_lv * c, c)), full((n_lv + 1, c, c))],
        out_specs=pl.BlockSpec((c, D_GRP), lambda b, k: (b * nc + k, 0)),
        out_shape=jax.ShapeDtypeStruct((n, D_GRP), BF16),
        scratch_shapes=[pltpu.VMEM((HGRN_HEADS, LANES, LANES), F32)],
        compiler_params=_cparams(("arbitrary", "arbitrary")),
        name="hgrn_mixer",
    )(proj, proj, proj, proj, lb.reshape(1, D_GRP).astype(F32),
      norm_g.reshape(1, LANES).astype(F32), tri, jnp.asarray(sums, BF16), jnp.asarray(masks, F32))


def _sb_block(q, kb, vb, upper, run, valid):
    z = _dot_nt(q, kb)
    ls = _log_sigmoid(z)
    lk = ls - z
    if valid is not None:
        lk = jnp.where(valid, lk, 0.0)
    between = run + _dot_exact_lhs(lk, upper)
    w = jnp.exp(ls + between)
    if valid is not None:
        w = jnp.where(valid, w, 0.0)
    return _dot(w, vb), run + jnp.sum(lk, axis=-1, keepdims=True)


def _sb_kernel(q_ref, k_ref, v_ref, upper_ref, o_ref):
    tq, tk = SB_TQ, SB_TK
    qi = pl.program_id(2)
    q = q_ref[...] * (SB_HEAD_DIM ** -0.5)
    upper = upper_ref[...]
    row = lax.broadcasted_iota(jnp.int32, (tq, tk), 0)
    col = lax.broadcasted_iota(jnp.int32, (tq, tk), 1)
    k0 = pl.multiple_of(qi * tq, tq)
    acc, run = _sb_block(q, k_ref[pl.ds(k0, tk), :], v_ref[pl.ds(k0, tk), :], upper,
                         jnp.zeros((tq, 1), F32), row > col)
    k1 = pl.multiple_of(jnp.maximum(qi - 1, 0) * tk, tk)
    d_acc, run = _sb_block(q, k_ref[pl.ds(k1, tk), :], v_ref[pl.ds(k1, tk), :], upper, run, qi > 0)
    acc = acc + d_acc

    def alive(run):
        return (jnp.max(run) > SB_DEAD_LOG).astype(jnp.int32)

    def cond(carry):
        j, go, _, _ = carry
        return jnp.logical_and(j >= 0, go > 0)

    def body(carry):
        j, _, acc, run = carry
        ks = pl.multiple_of(j * tk, tk)
        d_acc, run = _sb_block(q, k_ref[pl.ds(ks, tk), :], v_ref[pl.ds(ks, tk), :], upper, run,
                               None)
        return j - 1, alive(run), acc + d_acc, run

    _, _, acc, _ = lax.while_loop(cond, body, (qi - 2, alive(run), acc, run))
    o_ref[...] = acc


def _sb_mixer(proj, seq):
    n = proj.shape[0]
    bsz = n // seq
    tq, tk = SB_TQ, SB_TK
    nq = seq // tq
    upper = jnp.asarray(np.triu(np.ones((tk, tk), np.float32), 1).T, BF16)
    hpg = D_GRP // LANES
    return pl.pallas_call(
        _sb_kernel,
        grid=(bsz, SB_HEADS, nq),
        in_specs=[pl.BlockSpec((tq, LANES), lambda b, h, i: (b * nq + i, 8 * hpg + h)),
                  pl.BlockSpec((seq, LANES), lambda b, h, i: (b, 9 * hpg + h)),
                  pl.BlockSpec((seq, LANES), lambda b, h, i: (b, 10 * hpg + h)),
                  pl.BlockSpec((tk, tk), lambda b, h, i: (0, 0))],
        out_specs=pl.BlockSpec((tq, LANES), lambda b, h, i: (b * nq + i, h)),
        out_shape=jax.ShapeDtypeStruct((n, D_GRP), F32),
        compiler_params=_cparams(("arbitrary", "arbitrary", "arbitrary")),
        name="sb_mixer",
    )(proj, proj, proj, upper)


def _outproj_kernel(p0_ref, p1_ref, p2_ref, p3_ref, sbn_ref, w_ref, h_ref, gate_ref, o_ref):
    sb = p3_ref[...]
    sb = sb * lax.rsqrt(jnp.mean(sb * sb, axis=-1, keepdims=True) + EPS) * sbn_ref[...]
    acc = jnp.dot(p0_ref[...], w_ref[0, 0:D_GRP, :], preferred_element_type=F32)
    acc = acc + jnp.dot(p1_ref[...], w_ref[0, D_GRP:2 * D_GRP, :], preferred_element_type=F32)
    acc = acc + jnp.dot(p2_ref[...], w_ref[0, 2 * D_GRP:3 * D_GRP, :], preferred_element_type=F32)
    acc = acc + jnp.dot(sb.astype(BF16), w_ref[0, 3 * D_GRP:4 * D_GRP, :],
                        preferred_element_type=F32)
    o_ref[...] = h_ref[...] + gate_ref[0] * acc


def _out_proj(y_ssd, y_s5, y_hgrn, o_sb, sb_norm, layer, w_out, h2, gate, seq, tm=512, tn=1024):
    n, d = h2.shape
    bsz = gate.shape[0]
    tpb = seq // tm
    part = lambda: pl.BlockSpec((tm, D_GRP), lambda j, i: (i, 0))
    return pl.pallas_call(
        _outproj_kernel,
        grid=(d // tn, n // tm),
        in_specs=[part(), part(), part(), part(),
                  pl.BlockSpec((1, D_GRP), lambda j, i: (0, 0)),
                  pl.BlockSpec((1, 4 * D_GRP, tn), lambda j, i: (layer, 0, j)),
                  pl.BlockSpec((tm, tn), lambda j, i: (i, j)),
                  pl.BlockSpec((1, 1, tn), lambda j, i: (i // tpb, 0, j))],
        out_specs=pl.BlockSpec((tm, tn), lambda j, i: (i, j)),
        out_shape=jax.ShapeDtypeStruct((n, d), F32),
        compiler_params=_cparams(("arbitrary", "arbitrary")),
        name="out_proj",
    )(y_ssd, y_s5, y_hgrn, o_sb, sb_norm.reshape(1, D_GRP).astype(F32), w_out, h2,
      gate.reshape(bsz, 1, d))


def _router_kernel(x_ref, g_ref, sc_ref, sh_ref, w_ref, bias_ref, wsg_ref, wsu_ref, wsd_ref,
                   pk_ref, idx_ref, gate_ref, ysh_ref):
    x = x_ref[...]
    y = x * lax.rsqrt(jnp.mean(x * x, axis=-1, keepdims=True) + EPS) * g_ref[...]
    y = y * (1.0 + sc_ref[0]) + sh_ref[0]
    tm = y.shape[0]
    packed = _pack_rows(y)
    for s in range(PACK_S):
        pk_ref[pl.ds(s, tm, stride=PACK_S), :] = packed[:, s * LANES:(s + 1) * LANES]
    yb = y.astype(BF16)
    a = (_silu(jnp.dot(yb, wsg_ref[...], preferred_element_type=F32))
         * jnp.dot(yb, wsu_ref[...], preferred_element_type=F32)).astype(BF16)
    ysh_ref[...] = jnp.dot(a, wsd_ref[...], preferred_element_type=F32).astype(ysh_ref.dtype)
    y_lo = (y - yb.astype(F32)).astype(BF16)
    both = jnp.dot(yb, w_ref[...], preferred_element_type=F32)
    logits = (both[:, :LANES] + both[:, LANES:]
              + jnp.dot(y_lo, w_ref[:, :LANES], preferred_element_type=F32))
    scores = jax.nn.sigmoid(logits)
    lane = lax.broadcasted_iota(jnp.int32, (tm, LANES), 1)
    sel = jnp.where(lane < N_EXPERTS, scores + bias_ref[...], -jnp.inf)
    idx_out = jnp.zeros((tm, LANES), jnp.int32)
    gate_out = jnp.zeros((tm, LANES), F32)
    total = jnp.zeros((tm, 1), F32)
    for k in range(TOP_K):
        m = jnp.max(sel, axis=-1, keepdims=True)
        am = jnp.min(jnp.where(sel == m, lane, LANES), axis=-1, keepdims=True)
        hit = lane == am
        gk = jnp.sum(jnp.where(hit, scores, 0.0), axis=-1, keepdims=True)
        total = total + gk
        idx_out = jnp.where(lane == k, am, idx_out)
        gate_out = jnp.where(lane == k, gk, gate_out)
        sel = jnp.where(hit, -jnp.inf, sel)
    idx_ref[...] = idx_out
    gate_ref[...] = gate_out / total * ROUTED_SCALE


def _norm_router_shared(h2, g, scale, shift, seq, w_router, e_bias, ws_gate, ws_up, ws_down):
    n, d = h2.shape
    tm = ROUTER_TM
    bsz = scale.shape[0]
    tpb = seq // tm
    ff = ws_gate.shape[-1]
    w = jnp.zeros((d, LANES), F32).at[:, :N_EXPERTS].set(w_router.astype(F32))
    w_hi = w.astype(BF16)
    w = jnp.concatenate([w_hi, (w - w_hi.astype(F32)).astype(BF16)], axis=1)
    bias = jnp.zeros((1, LANES), F32).at[0, :N_EXPERTS].set(e_bias.astype(F32))
    const = lambda shape: pl.BlockSpec(shape, lambda i: (0,) * len(shape))
    return pl.pallas_call(
        _router_kernel,
        grid=(n // tm,),
        in_specs=[pl.BlockSpec((tm, d), lambda i: (i, 0)),
                  const((1, d)),
                  pl.BlockSpec((1, 1, d), lambda i: (i // tpb, 0, 0)),
                  pl.BlockSpec((1, 1, d), lambda i: (i // tpb, 0, 0)),
                  const((d, 2 * LANES)), const((1, LANES)),
                  const((d, ff)), const((d, ff)), const((ff, d))],
        out_specs=[pl.BlockSpec((tm * PACK_S, LANES), lambda i: (i, 0)),
                   pl.BlockSpec((tm, LANES), lambda i: (i, 0)),
                   pl.BlockSpec((tm, LANES), lambda i: (i, 0)),
                   pl.BlockSpec((tm, d), lambda i: (i, 0))],
        out_shape=[jax.ShapeDtypeStruct((n * PACK_S, LANES), jnp.uint32),
                   jax.ShapeDtypeStruct((n, LANES), jnp.int32),
                   jax.ShapeDtypeStruct((n, LANES), F32),
                   jax.ShapeDtypeStruct((n, d), BF16)],
        compiler_params=_cparams(("arbitrary",)),
        name="norm_router_shared",
    )(h2, g.reshape(1, d), scale.reshape(bsz, 1, d), shift.reshape(bsz, 1, d), w, bias,
      ws_gate.astype(BF16), ws_up.astype(BF16), ws_down.astype(BF16))


def _expert_kernel(be_ref, tab_ref, nused_ref, x_hbm, wg_ref, wu_ref, wd_ref, y_hbm,
                   xa, xb, xbf, ya, yb, sem_g, sem_s):
    rows = MOE_ROWS
    b = pl.program_id(0)
    last = nused_ref[0] - 1
    tok_mask = (1 << TOK_BITS) - 1
    kc = 1024

    def gather(blk, buf, sem):
        base = (blk + 1) * rows
        for r in range(rows):
            tok = tab_ref[base + r] & tok_mask
            pltpu.make_async_copy(x_hbm.at[pl.ds(tok * PACK_S, PACK_S), :],
                                  buf.at[pl.ds(r * BUF_PITCH, PACK_S), :], sem).start()

    def scatter(tab_blk, buf, sem):
        base = tab_blk * rows
        for r in range(rows):
            dst = pl.multiple_of((tab_ref[base + r] >> TOK_BITS) * SUBLANES, SUBLANES)
            pltpu.make_async_copy(buf.at[pl.ds(r * BUF_PITCH, PACK_S), :],
                                  y_hbm.at[pl.ds(dst, PACK_S), :], sem).start(priority=1)

    def wait_in(buf, sem):
        pltpu.make_async_copy(x_hbm.at[pl.ds(0, rows * PACK_S), :],
                              buf.at[pl.ds(0, rows * PACK_S), :], sem).wait()

    def wait_out(buf, sem):
        pltpu.make_async_copy(buf.at[pl.ds(0, rows * PACK_S), :],
                              y_hbm.at[pl.ds(0, rows * PACK_S), :], sem).wait()

    @pl.when(b == 0)
    def _():
        yb[...] = jnp.zeros_like(yb)
        gather(0, xa, sem_g.at[0])
        for half in range(2):
            spare = y_hbm.shape[0] - (2 - half) * rows * PACK_S
            fill = pltpu.make_async_copy(yb.at[pl.ds(0, rows * PACK_S), :],
                                         y_hbm.at[pl.ds(spare, rows * PACK_S), :], sem_s.at[0])
            fill.start()
            fill.wait()

    def step(cur_x, nxt_x, cur_y, prv_y, cur, nxt):
        wait_in(cur_x, sem_g.at[cur])

        @pl.when(b >= 1)
        def _():
            wait_out(cur_y, sem_s.at[cur])

        gather(jnp.minimum(b + 1, last), nxt_x, sem_g.at[nxt])
        scatter(b, prv_y, sem_s.at[nxt])
        for s in range(PACK_S):
            lo, hi = _unpack_words(cur_x[pl.ds(s, rows, stride=BUF_PITCH), :])
            xbf[:, s * LANES:(s + 1) * LANES] = lo.astype(BF16)
            xbf[:, PACK_W + s * LANES:PACK_W + (s + 1) * LANES] = hi.astype(BF16)
        g = u = None
        for c in range(xbf.shape[1] // kc):
            x_c = xbf[:, c * kc:(c + 1) * kc]
            g_c = jnp.dot(x_c, wg_ref[0, 0, c * kc:(c + 1) * kc, :].astype(BF16),
                          preferred_element_type=F32)
            u_c = jnp.dot(x_c, wu_ref[0, 0, c * kc:(c + 1) * kc, :].astype(BF16),
                          preferred_element_type=F32)
            g = g_c if g is None else g + g_c
            u = u_c if u is None else u + u_c
        a = (_silu(g) * u).astype(BF16)
        y = jnp.concatenate(
            [jnp.dot(a, wd_ref[0, 0, :, c * kc:(c + 1) * kc].astype(BF16),
                     preferred_element_type=F32) for c in range(xbf.shape[1] // kc)], axis=1)
        packed = _pack_rows(y)
        for s in range(PACK_S):
            cur_y[pl.ds(s, rows, stride=BUF_PITCH), :] = packed[:, s * LANES:(s + 1) * LANES]

        @pl.when(b == last)
        def _():
            scatter(b + 1, cur_y, sem_s.at[cur])
            wait_out(prv_y, sem_s.at[nxt])
            wait_out(cur_y, sem_s.at[cur])
            wait_in(nxt_x, sem_g.at[nxt])

    @pl.when(jnp.logical_and(b % 2 == 0, b <= last))
    def _():
        step(xa, xb, ya, yb, 0, 1)

    @pl.when(jnp.logical_and(b % 2 == 1, b <= last))
    def _():
        step(xb, xa, yb, ya, 1, 0)


def _routed_experts(x_packed, block_expert, table, n_used, n_y_rows, layer, w_gate, w_up, w_down):
    rows = MOE_ROWS
    n_blocks = block_expert.shape[0]
    d, ff = w_gate.shape[-2:]
    buf = lambda: pltpu.VMEM((rows * BUF_PITCH, LANES), jnp.uint32)
    w_idx = lambda b, be, tab, nu: (layer, be[jnp.minimum(b, nu[0] - 1)], 0, 0)
    grid_spec = pltpu.PrefetchScalarGridSpec(
        num_scalar_prefetch=3,
        grid=(n_blocks,),
        in_specs=[pl.BlockSpec(memory_space=pl.ANY),
                  pl.BlockSpec((1, 1, d, ff), w_idx),
                  pl.BlockSpec((1, 1, d, ff), w_idx),
                  pl.BlockSpec((1, 1, ff, d), w_idx)],
        out_specs=pl.BlockSpec(memory_space=pl.ANY),
        scratch_shapes=[buf(), buf(), pltpu.VMEM((rows, d), BF16), buf(), buf(),
                        pltpu.SemaphoreType.DMA((2,)), pltpu.SemaphoreType.DMA((2,))],
    )
    return pl.pallas_call(
        _expert_kernel,
        grid_spec=grid_spec,
        out_shape=jax.ShapeDtypeStruct((n_y_rows, LANES), jnp.uint32),
        compiler_params=_cparams(("arbitrary",), EXPERT_VMEM_LIMIT),
        name="routed_experts",
    )(block_expert, table, n_used, x_packed, w_gate, w_up, w_down)


def _combine_kernel(*refs, out_norm):
    y_refs = refs[:TOP_K]
    gates_ref, ysh_ref, h_ref, g2_ref, ng_ref, o_ref = refs[TOP_K:]
    tm = COMBINE_TM
    rt = 32
    for r0 in range(0, tm, rt):
        rsl = slice(r0, r0 + rt)
        gates = gates_ref[rsl, :]
        gk = [jnp.broadcast_to(gates[:, k:k + 1], (rt, LANES)) for k in range(TOP_K)]
        for s in range(PACK_S):
            lo_sl = slice(s * LANES, (s + 1) * LANES)
            hi_sl = slice(PACK_W + s * LANES, PACK_W + (s + 1) * LANES)
            acc_lo = ysh_ref[rsl, lo_sl].astype(F32)
            acc_hi = ysh_ref[rsl, hi_sl].astype(F32)
            for k in range(TOP_K):
                lo, hi = _unpack_words(y_refs[k][pl.ds(r0 * PACK_S + s, rt, stride=PACK_S), :])
                acc_lo = acc_lo + gk[k] * lo
                acc_hi = acc_hi + gk[k] * hi
            o_ref[rsl, lo_sl] = h_ref[rsl, lo_sl] + g2_ref[0, :, lo_sl] * acc_lo
            o_ref[rsl, hi_sl] = h_ref[rsl, hi_sl] + g2_ref[0, :, hi_sl] * acc_hi
    if out_norm:
        v = o_ref[...]
        o_ref[...] = v * lax.rsqrt(jnp.mean(v * v, axis=-1, keepdims=True) + EPS) * ng_ref[...]


def _combine(y_rows, gates, ysh, h2, gate2, seq, out_g):
    n, d = h2.shape
    tm = COMBINE_TM
    bsz = gate2.shape[0]
    tpb = seq // tm
    nt = n // tm
    slot = lambda k: pl.BlockSpec((tm * PACK_S, LANES), lambda i, k=k: (k * nt + i, 0))
    ng = jnp.ones((1, d), F32) if out_g is None else out_g.reshape(1, d).astype(F32)
    return pl.pallas_call(
        functools.partial(_combine_kernel, out_norm=out_g is not None),
        grid=(nt,),
        in_specs=[slot(k) for k in range(TOP_K)] + [
                  pl.BlockSpec((tm, LANES), lambda i: (i, 0)),
                  pl.BlockSpec((tm, d), lambda i: (i, 0)),
                  pl.BlockSpec((tm, d), lambda i: (i, 0)),
                  pl.BlockSpec((1, 1, d), lambda i: (i // tpb, 0, 0)),
                  pl.BlockSpec((1, d), lambda i: (0, 0))],
        out_specs=pl.BlockSpec((tm, d), lambda i: (i, 0)),
        out_shape=jax.ShapeDtypeStruct((n, d), F32),
        compiler_params=_cparams(("arbitrary",)),
        name="moe_combine",
    )(*([y_rows] * TOP_K), gates, ysh, h2, gate2.reshape(bsz, 1, d), ng)


def _routing_tables(idx, n):
    rows = MOE_ROWS
    n_pairs = n * TOP_K
    n_blocks = -(-(n_pairs + N_EXPERTS * (rows - 1)) // rows)
    e_flat = idx.reshape(-1)
    _, order = lax.sort_key_val(e_flat, jnp.arange(n_pairs, dtype=jnp.int32))
    experts = jnp.arange(N_EXPERTS, dtype=jnp.int32)
    counts = jnp.sum((e_flat[:, None] == experts[None, :]).astype(jnp.int32), axis=0)
    start = jnp.cumsum(counts) - counts
    padded = (counts + rows - 1) // rows * rows
    pad_end = jnp.cumsum(padded)
    pad_start = pad_end - padded
    block_start = jnp.arange(n_blocks, dtype=jnp.int32) * rows
    block_expert = jnp.minimum(
        jnp.sum((pad_end[None, :] <= block_start[:, None]).astype(jnp.int32), axis=1), N_EXPERTS - 1)
    onehot = (block_expert[:, None] == experts[None, :]).astype(jnp.int32)
    blk_pad_start = jnp.sum(onehot * pad_start[None, :], axis=1)
    blk_start = jnp.sum(onehot * start[None, :], axis=1)
    blk_count = jnp.sum(onehot * counts[None, :], axis=1)
    r_in = jnp.arange(rows, dtype=jnp.int32)[None, :]
    j = block_start[:, None] + r_in - blk_pad_start[:, None]
    valid = j < blk_count[:, None]
    src = jnp.clip(blk_start[:, None] + j, 0, n_pairs - 1)
    pair = jnp.take(order, src.reshape(-1), axis=0).reshape(n_blocks, rows)
    s8 = PACK_S // SUBLANES
    spare = (n_pairs + (jnp.arange(n_blocks, dtype=jnp.int32)[:, None] % 2) * rows + r_in) * s8
    tok = jnp.where(valid, pair // TOP_K, 0)
    dst = jnp.where(valid, ((pair % TOP_K) * n + pair // TOP_K) * s8, spare)
    table = (dst << TOK_BITS) | tok
    lead = ((n_pairs + rows + r_in) * s8) << TOK_BITS
    table = jnp.concatenate([lead, table], axis=0).reshape(-1).astype(jnp.int32)
    n_used = (pad_end[-1] // rows).astype(jnp.int32).reshape(1)
    return block_expert.astype(jnp.int32), table, n_used, (n_pairs + 2 * rows) * PACK_S


def _moe(h2, norm_g, scale2, shift2, gate2, seq, layer, w_router, e_bias, w_gate, w_up, w_down,
         ws_gate, ws_up, ws_down, out_g):
    n, d = h2.shape
    x_packed, idx, gates, ysh = _norm_router_shared(h2, norm_g, scale2, shift2, seq, w_router,
                                                    e_bias, ws_gate, ws_up, ws_down)
    block_expert, table, n_used, n_y_rows = _routing_tables(idx[:, :TOP_K], n)
    y_rows = _routed_experts(x_packed, block_expert, table, n_used, n_y_rows, layer,
                             w_gate, w_up, w_down)
    return _combine(y_rows, gates, ysh, h2, gate2, seq, out_g)


def _in_proj_weights(w_in):
    dt0 = 3 * D_GRP
    main = jnp.concatenate([w_in[:, :, :dt0], w_in[:, :, dt0 + SSD_HEADS:]], axis=2).astype(BF16)
    dtw = jnp.pad(w_in[:, :, dt0:dt0 + SSD_HEADS].astype(BF16),
                  ((0, 0), (0, 0), (0, LANES - SSD_HEADS)))
    return main, dtw


def kernel(x, c, w_ada, b_ada, ada_layer, norm1, w_in, ssd_conv_w, ssd_conv_b, ssd_dt_bias, ssd_a_log, ssd_d, ssd_norm, s5_lam_re, s5_lam_im, s5_log_step, s5_b_re, s5_b_im, s5_c_re, s5_c_im, s5_d, s5_w_glu, s5_b_glu, s5_norm, hgrn_lb_logits, hgrn_norm, sb_norm, w_out, norm2, w_router, e_bias, w_gate, w_up, w_down, ws_gate, ws_up, ws_down, final_norm):
    bsz, seq, d = x.shape
    n = bsz * seq
    depth = w_in.shape[0]
    mod = _ada_proj(c, w_ada, b_ada).reshape(bsz, N_MOD, d)
    lb_p = jax.nn.softmax(hgrn_lb_logits.astype(F32), axis=0)
    lower_bounds = jnp.cumsum(lb_p, axis=0) - lb_p[0]
    h = x.reshape(n, d)
    w_main, w_dt = _in_proj_weights(w_in)
    w_out_b = w_out.astype(BF16)
    for layer in range(depth):
        shift1, scale1, gate1, shift2, scale2, gate2 = [
            mod[:, j] + ada_layer[layer, j] for j in range(N_MOD)]
        proj, dt_raw = _in_proj(h, norm1[layer], scale1, shift1, seq, layer, w_main, w_dt)
        y_ssd = _ssd_mixer(proj, dt_raw, seq, ssd_conv_w[layer], ssd_conv_b[layer],
                           ssd_dt_bias[layer], ssd_a_log[layer], ssd_d[layer], ssd_norm[layer])
        y_s5 = _s5_mixer(proj, seq, s5_lam_re[layer], s5_lam_im[layer], s5_log_step[layer],
                         s5_b_re[layer], s5_b_im[layer], s5_c_re[layer], s5_c_im[layer],
                         s5_d[layer], s5_w_glu[layer], s5_b_glu[layer], s5_norm[layer])
        y_hgrn = _hgrn_mixer(proj, seq, lower_bounds[layer], hgrn_norm[layer])
        o_sb = _sb_mixer(proj, seq)
        h = _out_proj(y_ssd, y_s5, y_hgrn, o_sb, sb_norm[layer], layer, w_out_b, h, gate1, seq)
        h = _moe(h, norm2[layer], scale2, shift2, gate2, seq, layer, w_router[layer], e_bias[layer],
                 w_gate, w_up, w_down, ws_gate[layer], ws_up[layer], ws_down[layer],
                 final_norm if layer == depth - 1 else None)
    return h.reshape(bsz, seq, d)
```
